```python
import jax, jax.numpy as jnp
from jax import lax
import numpy as np

D_MODEL = 2048
BATCH = 8
SEQ = 2048
DEPTH = 4

N_MIXERS = 2
N_META = 16
NORM_EPS = 1e-6
L2_EPS = 1e-6
D_FF = 5504
FFN_RES = 0.5
DN_QK_HEADS = 16
DN_V_HEADS = 32
DN_HEAD_K = 128
DN_HEAD_V = 128
DN_CONV = 4
DN_CHUNK = 64
DN_KEY_DIM = DN_QK_HEADS * DN_HEAD_K
DN_VAL_DIM = DN_V_HEADS * DN_HEAD_V
DN_CONV_DIM = 2 * DN_KEY_DIM + DN_VAL_DIM
DN_PROJ = DN_CONV_DIM + DN_VAL_DIM + 2 * DN_V_HEADS
SWA_Q_HEADS = 32
SWA_KV_HEADS = 4
SWA_HEAD_DIM = 64
SWA_GROUP = SWA_Q_HEADS // SWA_KV_HEADS
SWA_WINDOW = 128
SWA_PROJ = (SWA_Q_HEADS + 2 * SWA_KV_HEADS) * SWA_HEAD_DIM
ROPE_THETA = 10000.0

kernel_name = "hybrid_gdn_swa_sink_macaron_meta"


def rms_norm(x, gain):
    xf = x.astype(jnp.float32)
    y = xf * lax.rsqrt(jnp.mean(xf * xf, axis=-1, keepdims=True) + NORM_EPS)
    return (y * gain.astype(jnp.float32)).astype(x.dtype)


def l2norm(x):
    xf = x.astype(jnp.float32)
    return xf * lax.rsqrt(jnp.sum(xf * xf, axis=-1, keepdims=True) + L2_EPS)


def swiglu(x, w_gu, w_down):
    g, u = jnp.split(x @ w_gu, 2, axis=-1)
    return (jax.nn.silu(g) * u) @ w_down


def rope(x, pos):
    hd = x.shape[-1]
    inv = ROPE_THETA ** (-jnp.arange(0, hd, 2, dtype=jnp.float32) / hd)
    ang = pos.astype(jnp.float32)[:, None] * inv[None, :]
    cos = jnp.cos(ang)[None, :, None, :]
    sin = jnp.sin(ang)[None, :, None, :]
    x1, x2 = jnp.split(x.astype(jnp.float32), 2, axis=-1)
    return jnp.concatenate([x1 * cos - x2 * sin, x2 * cos + x1 * sin], axis=-1).astype(x.dtype)


def causal_depthwise_conv(x, w):
    K, C = w.shape
    return lax.conv_general_dilated(
        x, w[:, None, :].astype(x.dtype), window_strides=(1,), padding=[(K - 1, 0)],
        dimension_numbers=('NWC', 'WIO', 'NWC'), feature_group_count=C)


def gated_deltanet(h, w_in, conv_w, a_log, dt_bias, o_gain, w_out):
    B, L, _ = h.shape
    C = DN_CHUNK
    proj = h @ w_in
    qkv, z, b, a = jnp.split(proj, [DN_CONV_DIM, DN_CONV_DIM + DN_VAL_DIM,
                                    DN_CONV_DIM + DN_VAL_DIM + DN_V_HEADS], axis=-1)
    qkv = jax.nn.silu(causal_depthwise_conv(qkv, conv_w))
    q, k, v = jnp.split(qkv, [DN_KEY_DIM, 2 * DN_KEY_DIM], axis=-1)
    q = l2norm(q.reshape(B, L, DN_QK_HEADS, DN_HEAD_K)) * (DN_HEAD_K ** -0.5)
    k = l2norm(k.reshape(B, L, DN_QK_HEADS, DN_HEAD_K))
    v = v.reshape(B, L, DN_V_HEADS, DN_HEAD_V).astype(jnp.float32)
    beta = jax.nn.sigmoid(b.astype(jnp.float32))
    g = -jnp.exp(a_log.astype(jnp.float32)) * jax.nn.softplus(
        a.astype(jnp.float32) + dt_bias.astype(jnp.float32))

    pad = (-L) % C

    def to_chunks(t):
        t = jnp.pad(t, [(0, 0), (pad, 0)] + [(0, 0)] * (t.ndim - 2))
        n = t.shape[1] // C
        t = t.reshape((B, n, C) + t.shape[2:])
        return jnp.moveaxis(t, (1, 2), (0, 3))

    xs = (to_chunks(q), to_chunks(k), to_chunks(v), to_chunks(g), to_chunks(beta))
    tri_incl = jnp.tril(jnp.ones((C, C), dtype=bool))
    tri_strict = jnp.tril(jnp.ones((C, C), dtype=bool), -1)
    eye = jnp.eye(C, dtype=jnp.float32)
    rep = DN_V_HEADS // DN_QK_HEADS

    def chunk_step(S, inp):
        qc, kc, vc, gc, bc = inp
        qc = jnp.repeat(qc, rep, axis=1)
        kc = jnp.repeat(kc, rep, axis=1)
        gcum = jnp.cumsum(gc, axis=-1)
        decay = jnp.exp(jnp.where(tri_incl, gcum[..., :, None] - gcum[..., None, :], -jnp.inf))
        kb = kc * bc[..., None]
        A = jnp.where(tri_strict, jnp.einsum('bhid,bhjd->bhij', kb, kc) * decay, 0.0)
        rhs = jnp.concatenate([vc * bc[..., None], kb * jnp.exp(gcum)[..., None]], axis=-1)
        sol = lax.linalg.triangular_solve(A + eye, rhs, left_side=True, lower=True)
        u, w = jnp.split(sol, [DN_HEAD_V], axis=-1)
        v_new = u - jnp.einsum('bhcd,bhde->bhce', w, S)
        attn = jnp.einsum('bhid,bhjd->bhij', qc, kc) * decay
        out = (jnp.einsum('bhcd,bhde->bhce', qc * jnp.exp(gcum)[..., None], S)
               + jnp.einsum('bhij,bhje->bhie', attn, v_new))
        g_last = gcum[..., -1:]
        S = (S * jnp.exp(g_last)[..., None]
             + jnp.einsum('bhcd,bhce->bhde', kc * jnp.exp(g_last - gcum)[..., None], v_new))
        return S, out

    S0 = jnp.zeros((B, DN_V_HEADS, DN_HEAD_K, DN_HEAD_V), jnp.float32)
    _, outs = lax.scan(chunk_step, S0, xs)
    o = jnp.moveaxis(outs, (0, 3), (1, 2)).reshape(B, -1, DN_V_HEADS, DN_HEAD_V)[:, pad:]
    zf = z.reshape(B, L, DN_V_HEADS, DN_HEAD_V).astype(jnp.float32)
    o = rms_norm(o, o_gain) * jax.nn.silu(zf)
    return o.reshape(B, L, DN_VAL_DIM).astype(h.dtype) @ w_out


def sliding_window_attention(h, w_qkv, sinks, w_out):
    B, L, _ = h.shape
    M, W, hd = N_META, SWA_WINDOW, SWA_HEAD_DIM
    T = L - M
    nb = T // W
    q, k, v = jnp.split(h @ w_qkv, [SWA_Q_HEADS * hd, (SWA_Q_HEADS + SWA_KV_HEADS) * hd], axis=-1)
    pos = jnp.arange(L, dtype=jnp.int32)
    q = rope(q.reshape(B, L, SWA_Q_HEADS, hd), pos).astype(jnp.float32) * (hd ** -0.5)
    q = q.reshape(B, L, SWA_KV_HEADS, SWA_GROUP, hd)
    k = rope(k.reshape(B, L, SWA_KV_HEADS, hd), pos).astype(jnp.float32)
    v = v.reshape(B, L, SWA_KV_HEADS, hd).astype(jnp.float32)
    qm, qr = q[:, :M], q[:, M:]
    km, kr = k[:, :M], k[:, M:]
    vm, vr = v[:, :M], v[:, M:]
    sink = sinks.astype(jnp.float32).reshape(SWA_KV_HEADS, SWA_GROUP)

    def band(xb):
        prev = jnp.pad(xb, [(0, 0), (1, 0), (0, 0), (0, 0), (0, 0)])[:, :-1]
        return jnp.concatenate([prev, xb], axis=2)

    qb = qr.reshape(B, nb, W, SWA_KV_HEADS, SWA_GROUP, hd)
    kband = band(kr.reshape(B, nb, W, SWA_KV_HEADS, hd))
    vband = band(vr.reshape(B, nb, W, SWA_KV_HEADS, hd))
    s_band = jnp.einsum('bnikgd,bnjkd->bkgnij', qb, kband)
    s_meta = jnp.einsum('bnikgd,bmkd->bkgnim', qb, km)
    ii = jnp.arange(W)[:, None]
    jj = jnp.arange(2 * W)[None, :]
    blk = jnp.arange(nb)[:, None, None]
    band_mask = (jj > ii) & (jj <= ii + W) & ((blk > 0) | (jj >= W))
    s_band = jnp.where(band_mask, s_band, -jnp.inf)
    s_sink = jnp.broadcast_to(sink[None, :, :, None, None, None], s_band.shape[:-1] + (1,))
    p = jax.nn.softmax(jnp.concatenate([s_band, s_meta, s_sink], axis=-1), axis=-1)
    o_r = (jnp.einsum('bkgnij,bnjkd->bnikgd', p[..., :2 * W], vband)
           + jnp.einsum('bkgnim,bmkd->bnikgd', p[..., 2 * W:2 * W + M], vm))
    o_r = o_r.reshape(B, T, SWA_Q_HEADS * hd)

    s_mm = jnp.einsum('bikgd,bjkd->bkgij', qm, km)
    s_mm = jnp.where(jnp.tril(jnp.ones((M, M), dtype=bool)), s_mm, -jnp.inf)
    s_msink = jnp.broadcast_to(sink[None, :, :, None, None], s_mm.shape[:-1] + (1,))
    p_m = jax.nn.softmax(jnp.concatenate([s_mm, s_msink], axis=-1), axis=-1)
    o_m = jnp.einsum('bkgij,bjkd->bikgd', p_m[..., :M], vm).reshape(B, M, SWA_Q_HEADS * hd)

    o = jnp.concatenate([o_m, o_r], axis=1).astype(h.dtype)
    return o @ w_out


def setup_inputs(seed: int = 0) -> dict:
    key = jax.random.key(seed)
    ks = jax.random.split(key, 24)
    n_dn = (DEPTH + N_MIXERS - 1) // N_MIXERS
    n_swa = DEPTH // N_MIXERS
    f32 = jnp.float32

    def nrm(k, shape, scale):
        return jax.random.normal(k, shape, f32) * scale

    def gain(k, shape):
        return 1.0 + 0.1 * jax.random.normal(k, shape, f32)

    dt = jnp.exp(jax.random.uniform(ks[13], (n_dn, DN_V_HEADS), f32, np.log(1e-3), np.log(1e-1)))
    return {
        "x": nrm(ks[0], (BATCH, SEQ, D_MODEL), 1.0),
        "meta_tokens": nrm(ks[1], (N_META, D_MODEL), 1.0),
        "ffn_pre_norm": gain(ks[2], (DEPTH, D_MODEL)),
        "ffn_pre_w_gu": nrm(ks[3], (DEPTH, D_MODEL, 2 * D_FF), D_MODEL ** -0.5),
        "ffn_pre_w_down": nrm(ks[4], (DEPTH, D_FF, D_MODEL), D_FF ** -0.5),
        "mix_norm": gain(ks[5], (DEPTH, D_MODEL)),
        "ffn_post_norm": gain(ks[6], (DEPTH, D_MODEL)),
        "ffn_post_w_gu": nrm(ks[7], (DEPTH, D_MODEL, 2 * D_FF), D_MODEL ** -0.5),
        "ffn_post_w_down": nrm(ks[8], (DEPTH, D_FF, D_MODEL), D_FF ** -0.5),
        "dn_w_in": nrm(ks[9], (n_dn, D_MODEL, DN_PROJ), D_MODEL ** -0.5),
        "dn_conv_w": nrm(ks[10], (n_dn, DN_CONV, DN_CONV_DIM), DN_CONV ** -0.5),
        "dn_a_log": jnp.log(jax.random.uniform(ks[11], (n_dn, DN_V_HEADS), f32, 1.0, 16.0)),
        "dn_dt_bias": dt + jnp.log(-jnp.expm1(-dt)),
        "dn_out_norm": gain(ks[12], (n_dn, DN_HEAD_V)),
        "dn_w_out": nrm(ks[14], (n_dn, DN_VAL_DIM, D_MODEL), DN_VAL_DIM ** -0.5),
        "swa_w_qkv": nrm(ks[15], (n_swa, D_MODEL, SWA_PROJ), D_MODEL ** -0.5),
        "swa_sinks": nrm(ks[16], (n_swa, SWA_Q_HEADS), 0.5),
        "swa_w_out": nrm(ks[17], (n_swa, SWA_Q_HEADS * SWA_HEAD_DIM, D_MODEL), (SWA_Q_HEADS * SWA_HEAD_DIM) ** -0.5),
        "final_norm": gain(ks[18], (D_MODEL,)),
    }


def reference(x, meta_tokens, ffn_pre_norm, ffn_pre_w_gu, ffn_pre_w_down, mix_norm,
              ffn_post_norm, ffn_post_w_gu, ffn_post_w_down, dn_w_in, dn_conv_w, dn_a_log,
              dn_dt_bias, dn_out_norm, dn_w_out, swa_w_qkv, swa_sinks, swa_w_out, final_norm):
    B = x.shape[0]
    meta = jnp.broadcast_to(meta_tokens[None].astype(x.dtype), (B, N_META, x.shape[-1]))
    h = jnp.concatenate([meta, x], axis=1)
    for i in range(DEPTH):
        j = i // N_MIXERS
        h = h + FFN_RES * swiglu(rms_norm(h, ffn_pre_norm[i]), ffn_pre_w_gu[i], ffn_pre_w_down[i])
        hn = rms_norm(h, mix_norm[i])
        if i % N_MIXERS == 0:
            h = h + gated_deltanet(hn, dn_w_in[j], dn_conv_w[j], dn_a_log[j], dn_dt_bias[j],
                                   dn_out_norm[j], dn_w_out[j])
        else:
            h = h + sliding_window_attention(hn, swa_w_qkv[j], swa_sinks[j], swa_w_out[j])
        h = h + FFN_RES * swiglu(rms_norm(h, ffn_post_norm[i]), ffn_post_w_gu[i], ffn_post_w_down[i])
    return rms_norm(h[:, N_META:], final_norm)
```

```python
import functools
import math

import jax
import jax.numpy as jnp
from jax import lax
from jax.experimental import pallas as pl
from jax.experimental.pallas import tpu as pltpu

F32 = jnp.float32
BF16 = jnp.bfloat16

N_META = 16
NORM_EPS = 1e-6
L2_EPS = 1e-6
FFN_RES = 0.5
DN_QK_HEADS = 16
DN_V_HEADS = 32
DN_HEAD = 128
DN_CONV = 4
DN_CHUNK = 64
DN_KEY_DIM = DN_QK_HEADS * DN_HEAD
DN_VAL_DIM = DN_V_HEADS * DN_HEAD
SWA_Q_HEADS = 32
SWA_KV_HEADS = 4
SWA_HEAD_DIM = 64
SWA_GROUP = SWA_Q_HEADS // SWA_KV_HEADS
SWA_WINDOW = 128
ROPE_THETA = 10000.0

LANES = 128
VMEM_LIMIT = 56 * 1024 * 1024
ROW_TILE_MAX = 688
FF_TILE = 512
DN_QK_PER_STEP = 2


def _cparams(*sem):
    return pltpu.CompilerParams(dimension_semantics=sem, vmem_limit_bytes=VMEM_LIMIT)


def _row_tile(rows):
    best = None
    for t in range(16, min(rows, ROW_TILE_MAX) + 1, 16):
        if rows % t == 0:
            best = t
    assert best is not None, rows
    return best


def _rms(x, gain):
    return x * lax.rsqrt(jnp.mean(x * x, axis=-1, keepdims=True) + NORM_EPS) * gain


def _silu(x):
    return x * jax.nn.sigmoid(x)


def _mm(a, b):
    return jnp.dot(a.astype(BF16), b.astype(BF16), preferred_element_type=F32)


def _ffn_kernel(h_ref, gain_ref, wg_ref, wu_ref, wd_ref, o_ref, xn_ref):
    j = pl.program_id(1)

    @pl.when(j == 0)
    def _():
        h = h_ref[...]
        xn_ref[...] = _rms(h, gain_ref[...]).astype(BF16)
        o_ref[...] = h

    xn = xn_ref[...]
    g = jnp.dot(xn, wg_ref[...], preferred_element_type=F32)
    u = jnp.dot(xn, wu_ref[...], preferred_element_type=F32)
    a = (_silu(g) * u).astype(BF16)
    o_ref[...] += FFN_RES * jnp.dot(a, wd_ref[...], preferred_element_type=F32)


def _ffn(h, gain, wgu, wd):
    rows, d = h.shape
    ffp = wd.shape[0]
    tm = _row_tile(rows)
    nj = ffp // FF_TILE
    return pl.pallas_call(
        _ffn_kernel,
        grid=(rows // tm, nj),
        in_specs=[
            pl.BlockSpec((tm, d), lambda i, j: (i, 0)),
            pl.BlockSpec((1, d), lambda i, j: (0, 0)),
            pl.BlockSpec((d, FF_TILE), lambda i, j: (0, j)),
            pl.BlockSpec((d, FF_TILE), lambda i, j: (0, j + nj)),
            pl.BlockSpec((FF_TILE, d), lambda i, j: (j, 0)),
        ],
        out_specs=pl.BlockSpec((tm, d), lambda i, j: (i, 0)),
        out_shape=jax.ShapeDtypeStruct((rows, d), F32),
        scratch_shapes=[pltpu.VMEM((tm, d), BF16)],
        compiler_params=_cparams("parallel", "arbitrary"),
        name="ffn",
    )(h, gain.reshape(1, d), wgu, wgu, wd)


def _prep_ffn_weights(w_gu, w_down):
    d_ff = w_down.shape[0]
    ffp = -(-d_ff // FF_TILE) * FF_TILE
    pad = ffp - d_ff
    wg = jnp.pad(w_gu[:, :d_ff], ((0, 0), (0, pad)))
    wu = jnp.pad(w_gu[:, d_ff:], ((0, 0), (0, pad)))
    wgu = jnp.concatenate([wg, wu], axis=1).astype(BF16)
    wd = jnp.pad(w_down, ((0, pad), (0, 0))).astype(BF16)
    return wgu, wd


def _proj_kernel(h_ref, gain_ref, w_ref, o_ref, xn_ref):
    @pl.when(pl.program_id(1) == 0)
    def _():
        xn_ref[...] = _rms(h_ref[...], gain_ref[...]).astype(BF16)

    o_ref[...] = jnp.dot(xn_ref[...], w_ref[...], preferred_element_type=F32).astype(o_ref.dtype)


def _norm_proj(h, gain, w, tn, out_dtype):
    rows, d = h.shape
    n = w.shape[1]
    tm = _row_tile(rows)
    return pl.pallas_call(
        _proj_kernel,
        grid=(rows // tm, n // tn),
        in_specs=[
            pl.BlockSpec((tm, d), lambda i, j: (i, 0)),
            pl.BlockSpec((1, d), lambda i, j: (0, 0)),
            pl.BlockSpec((d, tn), lambda i, j: (0, j)),
        ],
        out_specs=pl.BlockSpec((tm, tn), lambda i, j: (i, j)),
        out_shape=jax.ShapeDtypeStruct((rows, n), out_dtype),
        scratch_shapes=[pltpu.VMEM((tm, d), BF16)],
        compiler_params=_cparams("parallel", "arbitrary"),
        name="norm_proj",
    )(h, gain.reshape(1, d), w)


def _rope_table_kernel(c_ref, s1_ref, s2_ref):
    shape = c_ref.shape
    pos = lax.broadcasted_iota(jnp.int32, shape, 0).astype(F32)
    dim = lax.broadcasted_iota(jnp.int32, shape, 1) % SWA_HEAD_DIM
    half = SWA_HEAD_DIM // 2
    k = (dim % half).astype(F32)
    inv = jnp.exp(k * (-2.0 / SWA_HEAD_DIM * math.log(ROPE_THETA)))
    ang = pos * inv
    s = jnp.sin(ang)
    c_ref[...] = jnp.cos(ang)
    s1_ref[...] = jnp.where(dim < half, -s, 0.0)
    s2_ref[...] = jnp.where(dim >= half, s, 0.0)


def _rope_tables(seq):
    shp = jax.ShapeDtypeStruct((seq, LANES), F32)
    return pl.pallas_call(_rope_table_kernel, out_shape=(shp, shp, shp), name="rope_tables")()


def _swa_proj_kernel(h_ref, gain_ref, w_ref, c_ref, s1_ref, s2_ref, o_ref, *, n_rope):
    xn = _rms(h_ref[...], gain_ref[...]).astype(BF16)
    y = jnp.dot(xn, w_ref[...], preferred_element_type=F32)
    c, s1, s2 = c_ref[...], s1_ref[...], s2_ref[...]
    half = SWA_HEAD_DIM // 2
    for ch in range(y.shape[1] // LANES):
        yc = y[:, ch * LANES:(ch + 1) * LANES]
        if ch < n_rope:
            yc = (yc * c + pltpu.roll(yc, LANES - half, axis=1) * s1
                  + pltpu.roll(yc, half, axis=1) * s2)
        o_ref[:, ch * LANES:(ch + 1) * LANES] = yc.astype(o_ref.dtype)


def _swa_proj(h, gain, w, tables, seq):
    rows, d = h.shape
    n = w.shape[1]
    tm = _row_tile(seq)
    per_seq = seq // tm
    n_rope = (SWA_Q_HEADS + SWA_KV_HEADS) * SWA_HEAD_DIM // LANES
    tab = pl.BlockSpec((tm, LANES), lambda i: (i % per_seq, 0))
    return pl.pallas_call(
        functools.partial(_swa_proj_kernel, n_rope=n_rope),
        grid=(rows // tm,),
        in_specs=[
            pl.BlockSpec((tm, d), lambda i: (i, 0)),
            pl.BlockSpec((1, d), lambda i: (0, 0)),
            pl.BlockSpec((d, n), lambda i: (0, 0)),
            tab, tab, tab,
        ],
        out_specs=pl.BlockSpec((tm, n), lambda i: (i, 0)),
        out_shape=jax.ShapeDtypeStruct((rows, n), BF16),
        compiler_params=_cparams("parallel"),
        name="swa_proj",
    )(h, gain.reshape(1, d), w, *tables)


def _out_proj_kernel(a_ref, w_ref, h_ref, o_ref):
    o_ref[...] = h_ref[...] + jnp.dot(a_ref[...], w_ref[...], preferred_element_type=F32)


def _out_proj(a, w, h, tn=1024):
    rows, k = a.shape
    d = w.shape[1]
    tm = _row_tile(rows)
    return pl.pallas_call(
        _out_proj_kernel,
        grid=(rows // tm, d // tn),
        in_specs=[
            pl.BlockSpec((tm, k), lambda i, j: (i, 0)),
            pl.BlockSpec((k, tn), lambda i, j: (0, j)),
            pl.BlockSpec((tm, tn), lambda i, j: (i, j)),
        ],
        out_specs=pl.BlockSpec((tm, tn), lambda i, j: (i, j)),
        out_shape=jax.ShapeDtypeStruct((rows, d), F32),
        compiler_params=_cparams("parallel", "arbitrary"),
        name="out_proj",
    )(a, w, h)


def _swa_kernel(sink_ref, q_ref, kp_ref, kc_ref, km_ref, vp_ref, vc_ref, vm_ref, o_ref, *, seq):
    m = pl.program_id(1)
    w, hd, nm = SWA_WINDOW, SWA_HEAD_DIM, N_META
    nk = 2 * w + nm
    ii = lax.broadcasted_iota(jnp.int32, (w, nk), 0)
    jj = lax.broadcasted_iota(jnp.int32, (w, nk), 1)
    band = (jj > ii) & (jj <= ii + w) & ((m > 0) | (jj >= w))
    meta = (jj >= 2 * w) & (m * w + ii - (jj - 2 * w) >= w)
    mask = band | meta
    krow = lax.broadcasted_iota(jnp.int32, (nk, 1), 0)
    kabs = jnp.where(krow < 2 * w, (m - 1) * w + krow, krow - 2 * w)
    kvalid = (kabs >= 0) & (kabs < seq)
    k_all = jnp.concatenate([kp_ref[0], kc_ref[0], km_ref[0, :nm, :]], axis=0)
    v_all = jnp.concatenate([vp_ref[0], vc_ref[0], vm_ref[0, :nm, :]], axis=0)
    k_all = jnp.where(kvalid, k_all, jnp.zeros_like(k_all))
    v_all = jnp.where(kvalid, v_all, jnp.zeros_like(v_all))
    scale = hd ** -0.5
    for kv in range(SWA_KV_HEADS):
        kh = k_all[:, kv * hd:(kv + 1) * hd]
        vh = v_all[:, kv * hd:(kv + 1) * hd]
        outs = []
        for g in range(SWA_GROUP):
            h = kv * SWA_GROUP + g
            qh = q_ref[0, :, h * hd:(h + 1) * hd]
            s = lax.dot_general(qh, kh, (((1,), (1,)), ((), ())), preferred_element_type=F32)
            s = jnp.where(mask, s * scale, -jnp.inf)
            sink = sink_ref[h]
            mx = jnp.maximum(jnp.max(s, axis=-1, keepdims=True), sink)
            p = jnp.exp(s - mx)
            den = jnp.sum(p, axis=-1, keepdims=True) + jnp.exp(sink - mx)
            o = jnp.dot(p.astype(BF16), vh, preferred_element_type=F32)
            outs.append(o / den)
        for g in range(0, SWA_GROUP, 2):
            h = kv * SWA_GROUP + g
            o_ref[0, :, h * hd:(h + 2) * hd] = jnp.concatenate(outs[g:g + 2], axis=1).astype(o_ref.dtype)


def _swa_attention(qkv, sinks, batch, seq):
    n = qkv.shape[1]
    qkv = qkv.reshape(batch, seq, n)
    w = SWA_WINDOW
    nq = SWA_Q_HEADS * SWA_HEAD_DIM
    nkv = SWA_KV_HEADS * SWA_HEAD_DIM
    kcol = nq // nkv
    vcol = kcol + 1
    nblk = pl.cdiv(seq, w)

    def kv_spec(col, which):
        if which == "prev":
            return pl.BlockSpec((1, w, nkv), lambda b, m: (b, jnp.maximum(m - 1, 0), col))
        if which == "cur":
            return pl.BlockSpec((1, w, nkv), lambda b, m: (b, m, col))
        return pl.BlockSpec((1, w, nkv), lambda b, m: (b, 0, col))

    out = pl.pallas_call(
        functools.partial(_swa_kernel, seq=seq),
        grid=(batch, nblk),
        in_specs=[
            pl.BlockSpec(memory_space=pltpu.SMEM),
            pl.BlockSpec((1, w, nq), lambda b, m: (b, m, 0)),
            kv_spec(kcol, "prev"), kv_spec(kcol, "cur"), kv_spec(kcol, "meta"),
            kv_spec(vcol, "prev"), kv_spec(vcol, "cur"), kv_spec(vcol, "meta"),
        ],
        out_specs=pl.BlockSpec((1, w, nq), lambda b, m: (b, m, 0)),
        out_shape=jax.ShapeDtypeStruct((batch, seq, nq), BF16),
        compiler_params=_cparams("parallel", "parallel"),
        name="swa_attention",
    )(sinks.astype(F32), qkv, qkv, qkv, qkv, qkv, qkv, qkv)
    return out.reshape(batch * seq, nq)


def _tri_inverse(a, ii, jj):
    c = a.shape[0]
    eye = jnp.where(ii == jj, 1.0, 0.0)
    inv = eye - jnp.where((ii >> 1) == (jj >> 1), a, 0.0)
    lg = 1
    while (1 << lg) < c:
        same_pair = (ii >> (lg + 1)) == (jj >> (lg + 1))
        off = jnp.where(same_pair & ((ii >> lg) != (jj >> lg)), a, 0.0)
        inv = inv - _mm(inv, _mm(off, inv))
        lg += 1
    return inv


def _conv_silu(x_ref, tail_ref, w_ref):
    x = x_ref[0]
    c = x.shape[0]
    nt = tail_ref.shape[0]
    xe = jnp.concatenate([tail_ref[...], x], axis=0)
    w = w_ref[...]
    y = x * w[DN_CONV - 1:DN_CONV, :]
    for s in range(1, DN_CONV):
        y = y + pltpu.roll(xe, s, axis=0)[nt:, :] * w[DN_CONV - 1 - s:DN_CONV - s, :]
    tail_ref[...] = x[c - nt:, :]
    return _silu(y)


def _gdn_kernel(q_ref, k_ref, v_ref, z_ref, ba_ref, cwq_ref, cwk_ref, cwv_ref, alog_ref, dtb_ref,
                ogain_ref, o_ref, s_ref, tq_ref, tk_ref, tv_ref, *, nq, seq):
    c_idx = pl.program_id(2)
    nv = 2 * nq
    c, hd = DN_CHUNK, DN_HEAD

    @pl.when(c_idx == 0)
    def _():
        s_ref[...] = jnp.zeros_like(s_ref)
        tq_ref[...] = jnp.zeros_like(tq_ref)
        tk_ref[...] = jnp.zeros_like(tk_ref)
        tv_ref[...] = jnp.zeros_like(tv_ref)

    row = c_idx * c + lax.broadcasted_iota(jnp.int32, (c, 1), 0)
    valid = row < seq
    ii = lax.broadcasted_iota(jnp.int32, (c, c), 0)
    jj = lax.broadcasted_iota(jnp.int32, (c, c), 1)
    lower = ii >= jj

    qc = jnp.where(valid, _conv_silu(q_ref, tq_ref, cwq_ref), 0.0)
    kc = jnp.where(valid, _conv_silu(k_ref, tk_ref, cwk_ref), 0.0)
    vc = jnp.where(valid, _conv_silu(v_ref, tv_ref, cwv_ref), 0.0)

    ba = ba_ref[0]
    beta = jnp.where(valid, jax.nn.sigmoid(ba), 0.0)
    x = ba + dtb_ref[0]
    softplus = jnp.maximum(x, 0.0) + jnp.log1p(jnp.exp(-jnp.abs(x)))
    g = jnp.where(valid, -jnp.exp(alog_ref[0]) * softplus, 0.0)
    rr = lax.broadcasted_iota(jnp.int32, g.shape, 0)
    gc = g
    s = 1
    while s < c:
        gc = gc + jnp.where(rr >= s, pltpu.roll(gc, s, axis=0), 0.0)
        s *= 2
    gct = gc.T
    glast = gc[c - 1:c, :]

    for hq in range(nq):
        qh = qc[:, hq * hd:(hq + 1) * hd]
        kh = kc[:, hq * hd:(hq + 1) * hd]
        qh = qh * lax.rsqrt(jnp.sum(qh * qh, axis=-1, keepdims=True) + L2_EPS) * (hd ** -0.5)
        kh = kh * lax.rsqrt(jnp.sum(kh * kh, axis=-1, keepdims=True) + L2_EPS)
        kh16 = kh.astype(BF16)
        kk = lax.dot_general(kh16, kh16, (((1,), (1,)), ((), ())), preferred_element_type=F32)
        qk = lax.dot_general(qh.astype(BF16), kh16, (((1,), (1,)), ((), ())),
                             preferred_element_type=F32)
        for r in range(2):
            l = 2 * hq + r
            vh = vc[:, l * hd:(l + 1) * hd]
            bi = beta[:, l:l + 1]
            gi = gc[:, nv + l:nv + l + 1]
            gj = gct[nv + l:nv + l + 1, :]
            gl = glast[:, nv + l:nv + l + 1]
            decay = jnp.where(lower, jnp.exp(jnp.where(lower, gi - gj, 0.0)), 0.0)
            a = jnp.where(ii > jj, bi * kk * decay, 0.0)
            t = _tri_inverse(a, ii, jj)
            egi = jnp.exp(gi)
            kb = kh * bi
            sol = _mm(t, jnp.concatenate([vh * bi, kb * egi], axis=1))
            u, wmat = sol[:, :hd], sol[:, hd:]
            state = s_ref[l]
            v_new = u - _mm(wmat, state)
            out = _mm(qh * egi, state) + _mm(qk * decay, v_new)
            kd = (kh * jnp.exp(gl - gi)).astype(BF16)
            s_ref[l] = state * jnp.exp(gl) + lax.dot_general(
                kd, v_new.astype(BF16), (((0,), (0,)), ((), ())), preferred_element_type=F32)
            on = _rms(out, ogain_ref[...])
            zh = z_ref[0, :, l * hd:(l + 1) * hd]
            o_ref[0, :, l * hd:(l + 1) * hd] = (on * _silu(zh)).astype(o_ref.dtype)


def _gdn_mix(proj, conv_w, alog_g, dtb_g, o_gain, batch, seq):
    nq = DN_QK_PER_STEP
    nv = 2 * nq
    groups = DN_QK_HEADS // nq
    n = proj.shape[1]
    proj = proj.reshape(batch, seq, n)
    c = DN_CHUNK
    wq, wv = nq * DN_HEAD, nv * DN_HEAD
    k_off = DN_KEY_DIM // wq
    v_off = 2 * DN_KEY_DIM // wv
    z_off = (2 * DN_KEY_DIM + DN_VAL_DIM) // wv
    ba_off = (2 * DN_KEY_DIM + 2 * DN_VAL_DIM) // LANES
    tail = 8

    def col(width, off):
        return pl.BlockSpec((1, c, width), lambda b, g, t: (b, t, off + g))

    def cw(width, off):
        return pl.BlockSpec((DN_CONV, width), lambda b, g, t: (0, off + g))

    gate = pl.BlockSpec((1, 1, LANES), lambda b, g, t: (g, 0, 0))
    out = pl.pallas_call(
        functools.partial(_gdn_kernel, nq=nq, seq=seq),
        grid=(batch, groups, pl.cdiv(seq, c)),
        in_specs=[
            col(wq, 0), col(wq, k_off), col(wv, v_off), col(wv, z_off), col(LANES, ba_off),
            cw(wq, 0), cw(wq, k_off), cw(wv, v_off),
            gate, gate,
            pl.BlockSpec((1, DN_HEAD), lambda b, g, t: (0, 0)),
        ],
        out_specs=pl.BlockSpec((1, c, wv), lambda b, g, t: (b, t, g)),
        out_shape=jax.ShapeDtypeStruct((batch, seq, DN_VAL_DIM), BF16),
        scratch_shapes=[
            pltpu.VMEM((nv, DN_HEAD, DN_HEAD), F32),
            pltpu.VMEM((tail, wq), F32),
            pltpu.VMEM((tail, wq), F32),
            pltpu.VMEM((tail, wv), F32),
        ],
        compiler_params=_cparams("parallel", "parallel", "arbitrary"),
        name="gdn_mix",
    )(proj, proj, proj, proj, proj, conv_w, conv_w, conv_w, alog_g, dtb_g,
      o_gain.reshape(1, DN_HEAD))
    return out.reshape(batch * seq, DN_VAL_DIM)


def _prep_gdn_weights(w_in, a_log, dt_bias):
    nq = DN_QK_PER_STEP
    nv = 2 * nq
    groups = DN_QK_HEADS // nq
    d = w_in.shape[0]
    main = 2 * DN_KEY_DIM + 2 * DN_VAL_DIM
    w_b = w_in[:, main:main + DN_V_HEADS].reshape(d, groups, nv)
    w_a = w_in[:, main + DN_V_HEADS:].reshape(d, groups, nv)
    pad = jnp.zeros((d, groups, LANES - 2 * nv), w_in.dtype)
    w_ba = jnp.concatenate([w_b, w_a, pad], axis=2).reshape(d, groups * LANES)
    w_all = jnp.concatenate([w_in[:, :main], w_ba], axis=1).astype(BF16)

    def gate_rows(p):
        p = p.astype(F32).reshape(groups, nv)
        z = jnp.zeros((groups, nv), F32)
        zp = jnp.zeros((groups, LANES - 2 * nv), F32)
        return jnp.concatenate([z, p, zp], axis=1).reshape(groups, 1, LANES)

    return w_all, gate_rows(a_log), gate_rows(dt_bias)


def _final_norm_kernel(h_ref, gain_ref, o_ref):
    o_ref[...] = _rms(h_ref[...], gain_ref[...])


def _final_norm(h, gain):
    rows, d = h.shape
    tm = _row_tile(rows)
    return pl.pallas_call(
        _final_norm_kernel,
        grid=(rows // tm,),
        in_specs=[pl.BlockSpec((tm, d), lambda i: (i, 0)), pl.BlockSpec((1, d), lambda i: (0, 0))],
        out_specs=pl.BlockSpec((tm, d), lambda i: (i, 0)),
        out_shape=jax.ShapeDtypeStruct((rows, d), F32),
        compiler_params=_cparams("parallel"),
        name="final_norm",
    )(h, gain.reshape(1, d))


def kernel(x, meta_tokens, ffn_pre_norm, ffn_pre_w_gu, ffn_pre_w_down, mix_norm, ffn_post_norm,
           ffn_post_w_gu, ffn_post_w_down, dn_w_in, dn_conv_w, dn_a_log, dn_dt_bias, dn_out_norm,
           dn_w_out, swa_w_qkv, swa_sinks, swa_w_out, final_norm):
    batch, seq_in, d = x.shape
    seq = seq_in + N_META
    depth = ffn_pre_norm.shape[0]
    meta = jnp.broadcast_to(meta_tokens[None].astype(x.dtype), (batch, N_META, d))
    h = jnp.concatenate([meta, x], axis=1).reshape(batch * seq, d)
    tables = None
    for i in range(depth):
        j = i // 2
        h = _ffn(h, ffn_pre_norm[i], *_prep_ffn_weights(ffn_pre_w_gu[i], ffn_pre_w_down[i]))
        if i % 2 == 0:
            w_all, alog_g, dtb_g = _prep_gdn_weights(dn_w_in[j], dn_a_log[j], dn_dt_bias[j])
            proj = _norm_proj(h, mix_norm[i], w_all, 1024, F32)
            o = _gdn_mix(proj, dn_conv_w[j], alog_g, dtb_g, dn_out_norm[j], batch, seq)
            h = _out_proj(o, dn_w_out[j].astype(BF16), h)
        else:
            if tables is None:
                tables = _rope_tables(seq)
            qkv = _swa_proj(h, mix_norm[i], swa_w_qkv[j].astype(BF16), tables, seq)
            o = _swa_attention(qkv, swa_sinks[j], batch, seq)
            h = _out_proj(o, swa_w_out[j].astype(BF16), h)
        h = _ffn(h, ffn_post_norm[i], *_prep_ffn_weights(ffn_post_w_gu[i], ffn_post_w_down[i]))
    out = _final_norm(h, final_norm)
    return out.reshape(batch, seq, d)[:, N_META:]
```

```python
import functools
import math

import jax
import jax.numpy as jnp
from jax import lax
from jax.experimental import pallas as pl
from jax.experimental.pallas import tpu as pltpu

F32 = jnp.float32
BF16 = jnp.bfloat16

N_META = 16
NORM_EPS = 1e-6
L2_EPS = 1e-6
FFN_RES = 0.5
DN_QK_HEADS = 16
DN_V_HEADS = 32
DN_HEAD = 128
DN_CONV = 4
DN_CHUNK = 64
DN_KEY_DIM = DN_QK_HEADS * DN_HEAD
DN_VAL_DIM = DN_V_HEADS * DN_HEAD
SWA_Q_HEADS = 32
SWA_KV_HEADS = 4
SWA_HEAD_DIM = 64
SWA_GROUP = SWA_Q_HEADS // SWA_KV_HEADS
SWA_WINDOW = 128
ROPE_THETA = 10000.0

LANES = 128
VMEM_LIMIT = 56 * 1024 * 1024
ROW_TILE_MAX = 688
FF_TILE = 512
PROJ_TILE = 1024
DN_QK_PER_STEP = 8


def _cparams(*sem):
    return pltpu.CompilerParams(dimension_semantics=sem, vmem_limit_bytes=VMEM_LIMIT)


def _row_tile(rows):
    best = None
    for t in range(16, min(rows, ROW_TILE_MAX) + 1, 16):
        if rows % t == 0:
            best = t
    assert best is not None, rows
    return best


def _rms(x, gain):
    return x * lax.rsqrt(jnp.mean(x * x, axis=-1, keepdims=True) + NORM_EPS) * gain


def _silu(x):
    return x * jax.nn.sigmoid(x)


def _mm(a, b):
    return jnp.dot(a.astype(BF16), b.astype(BF16), preferred_element_type=F32)


def _ffn_kernel(h_ref, gain_ref, wg_ref, wu_ref, wd_ref, o_ref, xn_ref, *, d_ff):
    j = pl.program_id(1)

    @pl.when(j == 0)
    def _():
        h = h_ref[...]
        xn_ref[...] = _rms(h, gain_ref[...]).astype(BF16)
        o_ref[...] = h

    xn = xn_ref[...]
    g = jnp.dot(xn, wg_ref[...], preferred_element_type=F32)
    u = jnp.dot(xn, wu_ref[...], preferred_element_type=F32)
    a = _silu(g) * u
    wd = wd_ref[...]
    if d_ff % FF_TILE:
        col = j * FF_TILE + lax.broadcasted_iota(jnp.int32, (1, FF_TILE), 1)
        a = jnp.where(col < d_ff, a, 0.0)
        row = j * FF_TILE + lax.broadcasted_iota(jnp.int32, (FF_TILE, 1), 0)
        wd = jnp.where(row < d_ff, wd, jnp.zeros_like(wd))
    o_ref[...] += FFN_RES * jnp.dot(a.astype(BF16), wd, preferred_element_type=F32)


def _ffn(h, gain, wg, wu, wd):
    rows, d = h.shape
    d_ff = wd.shape[0]
    tm = _row_tile(rows)
    return pl.pallas_call(
        functools.partial(_ffn_kernel, d_ff=d_ff),
        grid=(rows // tm, pl.cdiv(d_ff, FF_TILE)),
        in_specs=[
            pl.BlockSpec((tm, d), lambda i, j: (i, 0)),
            pl.BlockSpec((1, d), lambda i, j: (0, 0)),
            pl.BlockSpec((d, FF_TILE), lambda i, j: (0, j)),
            pl.BlockSpec((d, FF_TILE), lambda i, j: (0, j)),
            pl.BlockSpec((FF_TILE, d), lambda i, j: (j, 0)),
        ],
        out_specs=pl.BlockSpec((tm, d), lambda i, j: (i, 0)),
        out_shape=jax.ShapeDtypeStruct((rows, d), F32),
        scratch_shapes=[pltpu.VMEM((tm, d), BF16)],
        compiler_params=_cparams("parallel", "arbitrary"),
        name="ffn",
    )(h, gain.reshape(1, d), wg, wu, wd)


def _prep_ffn_weights(w_gu, w_down):
    d_ff = w_down.shape[0]
    return w_gu[:, :d_ff].astype(BF16), w_gu[:, d_ff:].astype(BF16), w_down.astype(BF16)


def _proj_kernel(h_ref, gain_ref, w_ref, ws_ref, o_ref, os_ref, xn_ref):
    @pl.when(pl.program_id(1) == 0)
    def _():
        xn = _rms(h_ref[...], gain_ref[...]).astype(BF16)
        xn_ref[...] = xn
        os_ref[...] = jnp.dot(xn, ws_ref[...], preferred_element_type=F32)

    o_ref[...] = jnp.dot(xn_ref[...], w_ref[...], preferred_element_type=F32)


def _norm_proj(h, gain, w, w_small):
    rows, d = h.shape
    n, ns = w.shape[1], w_small.shape[1]
    tm = _row_tile(rows)
    tn = PROJ_TILE
    assert n % tn == 0
    return pl.pallas_call(
        _proj_kernel,
        grid=(rows // tm, n // tn),
        in_specs=[
            pl.BlockSpec((tm, d), lambda i, j: (i, 0)),
            pl.BlockSpec((1, d), lambda i, j: (0, 0)),
            pl.BlockSpec((d, tn), lambda i, j: (0, j)),
            pl.BlockSpec((d, ns), lambda i, j: (0, 0)),
        ],
        out_specs=[
            pl.BlockSpec((tm, tn), lambda i, j: (i, j)),
            pl.BlockSpec((tm, ns), lambda i, j: (i, 0)),
        ],
        out_shape=[
            jax.ShapeDtypeStruct((rows, n), F32),
            jax.ShapeDtypeStruct((rows, ns), F32),
        ],
        scratch_shapes=[pltpu.VMEM((tm, d), BF16)],
        compiler_params=_cparams("parallel", "arbitrary"),
        name="norm_proj",
    )(h, gain.reshape(1, d), w, w_small)


def _rope_table_kernel(c_ref, s1_ref, s2_ref):
    shape = c_ref.shape
    pos = lax.broadcasted_iota(jnp.int32, shape, 0).astype(F32)
    dim = lax.broadcasted_iota(jnp.int32, shape, 1) % SWA_HEAD_DIM
    half = SWA_HEAD_DIM // 2
    k = (dim % half).astype(F32)
    inv = jnp.exp(k * (-2.0 / SWA_HEAD_DIM * math.log(ROPE_THETA)))
    ang = pos * inv
    s = jnp.sin(ang)
    c_ref[...] = jnp.cos(ang)
    s1_ref[...] = jnp.where(dim < half, -s, 0.0)
    s2_ref[...] = jnp.where(dim >= half, s, 0.0)


def _rope_tables(seq):
    shp = jax.ShapeDtypeStruct((seq, LANES), F32)
    return pl.pallas_call(_rope_table_kernel, out_shape=(shp, shp, shp), name="rope_tables")()


def _swa_proj_kernel(h_ref, gain_ref, w_ref, c_ref, s1_ref, s2_ref, o_ref, *, n_rope):
    xn = _rms(h_ref[...], gain_ref[...]).astype(BF16)
    y = jnp.dot(xn, w_ref[...], preferred_element_type=F32)
    c, s1, s2 = c_ref[...], s1_ref[...], s2_ref[...]
    half = SWA_HEAD_DIM // 2
    for ch in range(y.shape[1] // LANES):
        yc = y[:, ch * LANES:(ch + 1) * LANES]
        if ch < n_rope:
            yc = (yc * c + pltpu.roll(yc, LANES - half, axis=1) * s1
                  + pltpu.roll(yc, half, axis=1) * s2)
        o_ref[:, ch * LANES:(ch + 1) * LANES] = yc.astype(o_ref.dtype)


def _swa_proj(h, gain, w, tables, seq):
    rows, d = h.shape
    n = w.shape[1]
    tm = _row_tile(seq)
    per_seq = seq // tm
    n_rope = (SWA_Q_HEADS + SWA_KV_HEADS) * SWA_HEAD_DIM // LANES
    tab = pl.BlockSpec((tm, LANES), lambda i: (i % per_seq, 0))
    return pl.pallas_call(
        functools.partial(_swa_proj_kernel, n_rope=n_rope),
        grid=(rows // tm,),
        in_specs=[
            pl.BlockSpec((tm, d), lambda i: (i, 0)),
            pl.BlockSpec((1, d), lambda i: (0, 0)),
            pl.BlockSpec((d, n), lambda i: (0, 0)),
            tab, tab, tab,
        ],
        out_specs=pl.BlockSpec((tm, n), lambda i: (i, 0)),
        out_shape=jax.ShapeDtypeStruct((rows, n), BF16),
        compiler_params=_cparams("parallel"),
        name="swa_proj",
    )(h, gain.reshape(1, d), w, *tables)


def _out_proj_kernel(a_ref, w_ref, h_ref, o_ref):
    o_ref[...] = h_ref[...] + jnp.dot(a_ref[...], w_ref[...], preferred_element_type=F32)


def _out_proj(a, w, h, tn=1024):
    rows, k = a.shape
    d = w.shape[1]
    tm = _row_tile(rows)
    return pl.pallas_call(
        _out_proj_kernel,
        grid=(rows // tm, d // tn),
        in_specs=[
            pl.BlockSpec((tm, k), lambda i, j: (i, 0)),
            pl.BlockSpec((k, tn), lambda i, j: (0, j)),
            pl.BlockSpec((tm, tn), lambda i, j: (i, j)),
        ],
        out_specs=pl.BlockSpec((tm, tn), lambda i, j: (i, j)),
        out_shape=jax.ShapeDtypeStruct((rows, d), F32),
        compiler_params=_cparams("parallel", "arbitrary"),
        name="out_proj",
    )(a, w, h)


def _swa_kernel(sink_ref, q_ref, kp_ref, kc_ref, km_ref, vp_ref, vc_ref, vm_ref, o_ref, *, seq):
    m = pl.program_id(1)
    w, hd, nm = SWA_WINDOW, SWA_HEAD_DIM, N_META
    nk = 2 * w + nm
    ii = lax.broadcasted_iota(jnp.int32, (w, nk), 0)
    jj = lax.broadcasted_iota(jnp.int32, (w, nk), 1)
    band = (jj > ii) & (jj <= ii + w) & ((m > 0) | (jj >= w))
    meta = (jj >= 2 * w) & (m * w + ii - (jj - 2 * w) >= w)
    mask = band | meta
    krow = lax.broadcasted_iota(jnp.int32, (nk, 1), 0)
    kabs = jnp.where(krow < 2 * w, (m - 1) * w + krow, krow - 2 * w)
    kvalid = (kabs >= 0) & (kabs < seq)
    k_all = jnp.concatenate([kp_ref[0], kc_ref[0], km_ref[0, :nm, :]], axis=0)
    v_all = jnp.concatenate([vp_ref[0], vc_ref[0], vm_ref[0, :nm, :]], axis=0)
    k_all = jnp.where(kvalid, k_all, jnp.zeros_like(k_all))
    v_all = jnp.where(kvalid, v_all, jnp.zeros_like(v_all))
    nh = SWA_Q_HEADS
    q = q_ref[0]
    q3 = jnp.stack([q[:, h * hd:(h + 1) * hd] for h in range(nh)], axis=0)
    k3 = jnp.stack([k_all[:, (h // SWA_GROUP) * hd:(h // SWA_GROUP + 1) * hd] for h in range(nh)], axis=0)
    v3 = jnp.stack([v_all[:, (h // SWA_GROUP) * hd:(h // SWA_GROUP + 1) * hd] for h in range(nh)], axis=0)
    s = jnp.einsum("hqd,hkd->hqk", q3, k3, preferred_element_type=F32)
    s = jnp.where(mask, s * (hd ** -0.5), -jnp.inf)
    sink = sink_ref[...]
    mx = jnp.maximum(jnp.max(s, axis=-1, keepdims=True), sink)
    p = jnp.exp(s - mx)
    den = jnp.sum(p, axis=-1, keepdims=True) + jnp.exp(sink - mx)
    o = jnp.einsum("hqk,hkd->hqd", p.astype(BF16), v3, preferred_element_type=F32) / den
    for h in range(0, nh, 2):
        o_ref[0, :, h * hd:(h + 2) * hd] = jnp.concatenate([o[h], o[h + 1]], axis=1).astype(o_ref.dtype)


def _swa_attention(qkv, sinks, batch, seq):
    n = qkv.shape[1]
    qkv = qkv.reshape(batch, seq, n)
    w = SWA_WINDOW
    nq = SWA_Q_HEADS * SWA_HEAD_DIM
    nkv = SWA_KV_HEADS * SWA_HEAD_DIM
    kcol = nq // nkv
    vcol = kcol + 1
    nblk = pl.cdiv(seq, w)

    def kv_spec(col, which):
        if which == "prev":
            return pl.BlockSpec((1, w, nkv), lambda b, m: (b, jnp.maximum(m - 1, 0), col))
        if which == "cur":
            return pl.BlockSpec((1, w, nkv), lambda b, m: (b, m, col))
        return pl.BlockSpec((1, w, nkv), lambda b, m: (b, 0, col))

    out = pl.pallas_call(
        functools.partial(_swa_kernel, seq=seq),
        grid=(batch, nblk),
        in_specs=[
            pl.BlockSpec((SWA_Q_HEADS, 1, 1), lambda b, m: (0, 0, 0)),
            pl.BlockSpec((1, w, nq), lambda b, m: (b, m, 0)),
            kv_spec(kcol, "prev"), kv_spec(kcol, "cur"), kv_spec(kcol, "meta"),
            kv_spec(vcol, "prev"), kv_spec(vcol, "cur"), kv_spec(vcol, "meta"),
        ],
        out_specs=pl.BlockSpec((1, w, nq), lambda b, m: (b, m, 0)),
        out_shape=jax.ShapeDtypeStruct((batch, seq, nq), BF16),
        compiler_params=_cparams("parallel", "parallel"),
        name="swa_attention",
    )(sinks.astype(F32).reshape(SWA_Q_HEADS, 1, 1), qkv, qkv, qkv, qkv, qkv, qkv, qkv)
    return out.reshape(batch * seq, nq)


def _bmm(a, b):
    return jnp.einsum("hmk,hkn->hmn", a.astype(BF16), b.astype(BF16), preferred_element_type=F32)


def _bmm_nt(a, b):
    return jnp.einsum("hmk,hnk->hmn", a.astype(BF16), b.astype(BF16), preferred_element_type=F32)


def _bmm_tn(a, b):
    return jnp.einsum("hkm,hkn->hmn", a.astype(BF16), b.astype(BF16), preferred_element_type=F32)


def _tri_inverse(a, ii, jj):
    c = a.shape[-1]
    eye = jnp.where(ii == jj, 1.0, 0.0)
    inv = eye - jnp.where((ii >> 1) == (jj >> 1), a, 0.0)
    lg = 1
    while (1 << lg) < c:
        same_pair = (ii >> (lg + 1)) == (jj >> (lg + 1))
        off = jnp.where(same_pair & ((ii >> lg) != (jj >> lg)), a, 0.0)
        inv = inv - _bmm(inv, _bmm(off, inv))
        lg += 1
    return inv


def _conv_silu(x_ref, tail_ref, w_ref):
    x = x_ref[0]
    c = x.shape[0]
    nt = tail_ref.shape[0]
    xe = jnp.concatenate([tail_ref[...], x], axis=0)
    w = w_ref[...]
    y = x * w[DN_CONV - 1:DN_CONV, :]
    for s in range(1, DN_CONV):
        y = y + pltpu.roll(xe, s, axis=0)[nt:, :] * w[DN_CONV - 1 - s:DN_CONV - s, :]
    tail_ref[...] = x[c - nt:, :]
    return _silu(y)


def _gdn_kernel(q_ref, k_ref, v_ref, z_ref, ba_ref, cwq_ref, cwk_ref, cwv_ref, alog_ref, dtb_ref,
                ogain_ref, o_ref, s_ref, tq_ref, tk_ref, tv_ref, *, nq, seq):
    c_idx = pl.program_id(2)
    nv = 2 * nq
    c, hd = DN_CHUNK, DN_HEAD

    @pl.when(c_idx == 0)
    def _():
        s_ref[...] = jnp.zeros_like(s_ref)
        tq_ref[...] = jnp.zeros_like(tq_ref)
        tk_ref[...] = jnp.zeros_like(tk_ref)
        tv_ref[...] = jnp.zeros_like(tv_ref)

    row = c_idx * c + lax.broadcasted_iota(jnp.int32, (c, 1), 0)
    valid = row < seq
    ii = lax.broadcasted_iota(jnp.int32, (c, c), 0)
    jj = lax.broadcasted_iota(jnp.int32, (c, c), 1)
    lower = ii >= jj

    qc = jnp.where(valid, _conv_silu(q_ref, tq_ref, cwq_ref), 0.0)
    kc = jnp.where(valid, _conv_silu(k_ref, tk_ref, cwk_ref), 0.0)
    vc = jnp.where(valid, _conv_silu(v_ref, tv_ref, cwv_ref), 0.0)

    ba = ba_ref[0]
    beta = jnp.where(valid, jax.nn.sigmoid(ba), 0.0)
    x = ba + dtb_ref[0]
    softplus = jnp.maximum(x, 0.0) + jnp.log1p(jnp.exp(-jnp.abs(x)))
    g = jnp.where(valid, -jnp.exp(alog_ref[0]) * softplus, 0.0)
    rr = lax.broadcasted_iota(jnp.int32, g.shape, 0)
    gc = g
    s = 1
    while s < c:
        gc = gc + jnp.where(rr >= s, pltpu.roll(gc, s, axis=0), 0.0)
        s *= 2
    gct = gc.T
    glast = gc[c - 1:c, :]

    def heads(x, n):
        return jnp.stack([x[:, h * hd:(h + 1) * hd] for h in range(n)], axis=0)

    def per_v_head(x):
        return jnp.stack([x[l // 2] for l in range(nv)], axis=0)

    qh = heads(qc, nq)
    kh = heads(kc, nq)
    qh = qh * lax.rsqrt(jnp.sum(qh * qh, axis=-1, keepdims=True) + L2_EPS) * (hd ** -0.5)
    kh = kh * lax.rsqrt(jnp.sum(kh * kh, axis=-1, keepdims=True) + L2_EPS)
    kk = per_v_head(_bmm_nt(kh, kh))
    qk = per_v_head(_bmm_nt(qh, kh))
    qv = per_v_head(qh)
    kv = per_v_head(kh)
    vh = heads(vc, nv)
    zh = heads(z_ref[0], nv)
    bi = jnp.stack([beta[:, l:l + 1] for l in range(nv)], axis=0)
    gi = jnp.stack([gc[:, nv + l:nv + l + 1] for l in range(nv)], axis=0)
    gj = jnp.stack([gct[nv + l:nv + l + 1, :] for l in range(nv)], axis=0)
    gl = jnp.stack([glast[:, nv + l:nv + l + 1] for l in range(nv)], axis=0)

    decay = jnp.where(lower, jnp.exp(jnp.where(lower, gi - gj, 0.0)), 0.0)
    a = jnp.where(ii > jj, bi * kk * decay, 0.0)
    t = _tri_inverse(a, ii, jj)
    egi = jnp.exp(gi)
    kb = kv * bi
    sol = _bmm(t, jnp.concatenate([vh * bi, kb * egi], axis=-1))
    u, wmat = sol[:, :, :hd], sol[:, :, hd:]
    state = s_ref[...]
    v_new = u - _bmm(wmat, state)
    out = _bmm(qv * egi, state) + _bmm(qk * decay, v_new)
    s_ref[...] = state * jnp.exp(gl) + _bmm_tn(kv * jnp.exp(gl - gi), v_new)
    o = _rms(out, ogain_ref[...]) * _silu(zh)
    for l in range(nv):
        o_ref[0, :, l * hd:(l + 1) * hd] = o[l].astype(o_ref.dtype)


def _gdn_mix(proj, ba, conv_w, alog_g, dtb_g, o_gain, batch, seq):
    nq = DN_QK_PER_STEP
    nv = 2 * nq
    groups = DN_QK_HEADS // nq
    proj = proj.reshape(batch, seq, proj.shape[1])
    ba = ba.reshape(batch, seq, ba.shape[1])
    c = DN_CHUNK
    wq, wv = nq * DN_HEAD, nv * DN_HEAD
    k_off = DN_KEY_DIM // wq
    v_off = 2 * DN_KEY_DIM // wv
    z_off = (2 * DN_KEY_DIM + DN_VAL_DIM) // wv
    tail = 8

    def col(width, off):
        return pl.BlockSpec((1, c, width), lambda b, g, t: (b, t, off + g))

    def cw(width, off):
        return pl.BlockSpec((DN_CONV, width), lambda b, g, t: (0, off + g))

    gate = pl.BlockSpec((1, 1, LANES), lambda b, g, t: (g, 0, 0))
    out = pl.pallas_call(
        functools.partial(_gdn_kernel, nq=nq, seq=seq),
        grid=(batch, groups, pl.cdiv(seq, c)),
        in_specs=[
            col(wq, 0), col(wq, k_off), col(wv, v_off), col(wv, z_off), col(LANES, 0),
            cw(wq, 0), cw(wq, k_off), cw(wv, v_off),
            gate, gate,
            pl.BlockSpec((1, DN_HEAD), lambda b, g, t: (0, 0)),
        ],
        out_specs=pl.BlockSpec((1, c, wv), lambda b, g, t: (b, t, g)),
        out_shape=jax.ShapeDtypeStruct((batch, seq, DN_VAL_DIM), BF16),
        scratch_shapes=[
            pltpu.VMEM((nv, DN_HEAD, DN_HEAD), F32),
            pltpu.VMEM((tail, wq), F32),
            pltpu.VMEM((tail, wq), F32),
            pltpu.VMEM((tail, wv), F32),
        ],
        compiler_params=_cparams("parallel", "parallel", "arbitrary"),
        name="gdn_mix",
    )(proj, proj, proj, proj, ba, conv_w, conv_w, conv_w, alog_g, dtb_g,
      o_gain.reshape(1, DN_HEAD))
    return out.reshape(batch * seq, DN_VAL_DIM)


def _prep_gdn_weights(w_in, a_log, dt_bias):
    nq = DN_QK_PER_STEP
    nv = 2 * nq
    groups = DN_QK_HEADS // nq
    d = w_in.shape[0]
    main = 2 * DN_KEY_DIM + 2 * DN_VAL_DIM
    w_b = w_in[:, main:main + DN_V_HEADS].reshape(d, groups, nv)
    w_a = w_in[:, main + DN_V_HEADS:].reshape(d, groups, nv)
    pad = jnp.zeros((d, groups, LANES - 2 * nv), w_in.dtype)
    w_ba = jnp.concatenate([w_b, w_a, pad], axis=2).reshape(d, groups * LANES).astype(BF16)
    w_main = w_in[:, :main].astype(BF16)

    def gate_rows(p):
        p = p.astype(F32).reshape(groups, nv)
        z = jnp.zeros((groups, nv), F32)
        zp = jnp.zeros((groups, LANES - 2 * nv), F32)
        return jnp.concatenate([z, p, zp], axis=1).reshape(groups, 1, LANES)

    return w_main, w_ba, gate_rows(a_log), gate_rows(dt_bias)


def _final_norm_kernel(h_ref, gain_ref, o_ref):
    o_ref[...] = _rms(h_ref[0], gain_ref[...])


def _final_norm(h, gain, batch, seq):
    d = h.shape[1]
    h = h.reshape(batch, seq, d)
    n_out = seq - N_META
    tr = 256 if n_out % 256 == 0 else n_out
    return pl.pallas_call(
        _final_norm_kernel,
        grid=(batch, n_out // tr),
        in_specs=[
            pl.BlockSpec((pl.Element(1), pl.Element(tr), pl.Element(d)),
                         lambda b, r: (b, pl.multiple_of(N_META + r * tr, 8), 0)),
            pl.BlockSpec((1, d), lambda b, r: (0, 0)),
        ],
        out_specs=pl.BlockSpec((None, tr, d), lambda b, r: (b, r, 0)),
        out_shape=jax.ShapeDtypeStruct((batch, n_out, d), F32),
        compiler_params=_cparams("parallel", "parallel"),
        name="final_norm",
    )(h, gain.reshape(1, d))


def kernel(x, meta_tokens, ffn_pre_norm, ffn_pre_w_gu, ffn_pre_w_down, mix_norm, ffn_post_norm,
           ffn_post_w_gu, ffn_post_w_down, dn_w_in, dn_conv_w, dn_a_log, dn_dt_bias, dn_out_norm,
           dn_w_out, swa_w_qkv, swa_sinks, swa_w_out, final_norm):
    batch, seq_in, d = x.shape
    seq = seq_in + N_META
    depth = ffn_pre_norm.shape[0]
    meta = jnp.broadcast_to(meta_tokens[None].astype(x.dtype), (batch, N_META, d))
    h = jnp.concatenate([meta, x], axis=1).reshape(batch * seq, d)
    tables = None
    for i in range(depth):
        j = i // 2
        h = _ffn(h, ffn_pre_norm[i], *_prep_ffn_weights(ffn_pre_w_gu[i], ffn_pre_w_down[i]))
        if i % 2 == 0:
            w_main, w_ba, alog_g, dtb_g = _prep_gdn_weights(dn_w_in[j], dn_a_log[j], dn_dt_bias[j])
            proj, ba = _norm_proj(h, mix_norm[i], w_main, w_ba)
            o = _gdn_mix(proj, ba, dn_conv_w[j], alog_g, dtb_g, dn_out_norm[j], batch, seq)
            h = _out_proj(o, dn_w_out[j].astype(BF16), h)
        else:
            if tables is None:
                tables = _rope_tables(seq)
            qkv = _swa_proj(h, mix_norm[i], swa_w_qkv[j].astype(BF16), tables, seq)
            o = _swa_attention(qkv, swa_sinks[j], batch, seq)
            h = _out_proj(o, swa_w_out[j].astype(BF16), h)
        h = _ffn(h, ffn_post_norm[i], *_prep_ffn_weights(ffn_post_w_gu[i], ffn_post_w_down[i]))
    return _final_norm(h, final_norm, batch, seq)
```

```python
import functools
import math

import jax
import jax.numpy as jnp
from jax import lax
from jax.experimental import pallas as pl
from jax.experimental.pallas import tpu as pltpu

F32 = jnp.float32
BF16 = jnp.bfloat16

N_META = 16
NORM_EPS = 1e-6
L2_EPS = 1e-6
FFN_RES = 0.5
DN_QK_HEADS = 16
DN_V_HEADS = 32
DN_HEAD = 128
DN_CONV = 4
DN_CHUNK = 64
DN_KEY_DIM = DN_QK_HEADS * DN_HEAD
DN_VAL_DIM = DN_V_HEADS * DN_HEAD
SWA_Q_HEADS = 32
SWA_KV_HEADS = 4
SWA_HEAD_DIM = 64
SWA_GROUP = SWA_Q_HEADS // SWA_KV_HEADS
SWA_WINDOW = 128
ROPE_THETA = 10000.0

LANES = 128
VMEM_LIMIT = 56 * 1024 * 1024
ROW_TILE_MAX = 688
FF_TILE = 512
PROJ_TILE = 2048
CAST_TILE_BYTES = 6 * 1024 * 1024
CONV_HALO = 8
DN_QK_PER_STEP = 16


def _cparams(*sem):
    return pltpu.CompilerParams(dimension_semantics=sem, vmem_limit_bytes=VMEM_LIMIT)


def _row_tile(rows):
    best = None
    for t in range(16, min(rows, ROW_TILE_MAX) + 1, 16):
        if rows % t == 0:
            best = t
    assert best is not None, rows
    return best


def _rms(x, gain):
    return x * lax.rsqrt(jnp.mean(x * x, axis=-1, keepdims=True) + NORM_EPS) * gain


def _silu(x):
    return x * jax.nn.sigmoid(x)


def _mm(a, b):
    return jnp.dot(a.astype(BF16), b.astype(BF16), preferred_element_type=F32)


def _ffn_kernel(h_ref, gain_ref, wg_ref, wu_ref, wd_ref, o_ref, xn_ref, *, d_ff):
    j = pl.program_id(1)

    @pl.when(j == 0)
    def _():
        h = h_ref[...]
        xn_ref[...] = _rms(h, gain_ref[...]).astype(BF16)
        o_ref[...] = h

    xn = xn_ref[...]
    g = jnp.dot(xn, wg_ref[...], preferred_element_type=F32)
    u = jnp.dot(xn, wu_ref[...], preferred_element_type=F32)
    a = _silu(g) * u
    wd = wd_ref[...]
    if d_ff % FF_TILE:
        col = j * FF_TILE + lax.broadcasted_iota(jnp.int32, (1, FF_TILE), 1)
        a = jnp.where(col < d_ff, a, 0.0)
        row = j * FF_TILE + lax.broadcasted_iota(jnp.int32, (FF_TILE, 1), 0)
        wd = jnp.where(row < d_ff, wd, jnp.zeros_like(wd))
    o_ref[...] += FFN_RES * jnp.dot(a.astype(BF16), wd, preferred_element_type=F32)


def _ffn(h, gain, wg, wu, wd):
    rows, d = h.shape
    d_ff = wd.shape[0]
    tm = _row_tile(rows)
    return pl.pallas_call(
        functools.partial(_ffn_kernel, d_ff=d_ff),
        grid=(rows // tm, pl.cdiv(d_ff, FF_TILE)),
        in_specs=[
            pl.BlockSpec((tm, d), lambda i, j: (i, 0)),
            pl.BlockSpec((1, d), lambda i, j: (0, 0)),
            pl.BlockSpec((d, FF_TILE), lambda i, j: (0, j)),
            pl.BlockSpec((d, FF_TILE), lambda i, j: (0, j)),
            pl.BlockSpec((FF_TILE, d), lambda i, j: (j, 0)),
        ],
        out_specs=pl.BlockSpec((tm, d), lambda i, j: (i, 0)),
        out_shape=jax.ShapeDtypeStruct((rows, d), F32),
        scratch_shapes=[pltpu.VMEM((tm, d), BF16)],
        compiler_params=_cparams("parallel", "arbitrary"),
        name="ffn",
    )(h, gain.reshape(1, d), wg, wu, wd)


def _cast_kernel(w_ref, o_ref):
    o_ref[...] = w_ref[...].astype(o_ref.dtype)


def _cast_bf16(w, layer, col_block, n_cols):
    rows = w.shape[1]
    tr = None
    for t in range(16, rows + 1, 16):
        if rows % t == 0 and t * n_cols * 4 <= CAST_TILE_BYTES:
            tr = t
    assert tr is not None, (rows, n_cols)
    return pl.pallas_call(
        _cast_kernel,
        grid=(rows // tr,),
        in_specs=[pl.BlockSpec((None, tr, n_cols), lambda r: (layer, r, col_block))],
        out_specs=pl.BlockSpec((tr, n_cols), lambda r: (r, 0)),
        out_shape=jax.ShapeDtypeStruct((rows, n_cols), BF16),
        compiler_params=_cparams("parallel"),
        name="cast_bf16",
    )(w)


def _prep_ffn_weights(w_gu, w_down, layer):
    d_ff = w_down.shape[1]
    return (_cast_bf16(w_gu, layer, 0, d_ff), _cast_bf16(w_gu, layer, 1, d_ff),
            _cast_bf16(w_down, layer, 0, w_down.shape[2]))


def _proj_kernel(h_ref, gain_ref, w_ref, ws_ref, o_ref, os_ref, xn_ref):
    @pl.when(pl.program_id(1) == 0)
    def _():
        xn = _rms(h_ref[...], gain_ref[...]).astype(BF16)
        xn_ref[...] = xn
        os_ref[...] = jnp.dot(xn, ws_ref[...], preferred_element_type=F32)

    o_ref[...] = jnp.dot(xn_ref[...], w_ref[...], preferred_element_type=F32)


def _norm_proj(h, gain, w, w_small):
    rows, d = h.shape
    n, ns = w.shape[1], w_small.shape[1]
    tm = _row_tile(rows)
    tn = PROJ_TILE
    assert n % tn == 0
    return pl.pallas_call(
        _proj_kernel,
        grid=(rows // tm, n // tn),
        in_specs=[
            pl.BlockSpec((tm, d), lambda i, j: (i, 0)),
            pl.BlockSpec((1, d), lambda i, j: (0, 0)),
            pl.BlockSpec((d, tn), lambda i, j: (0, j)),
            pl.BlockSpec((d, ns), lambda i, j: (0, 0)),
        ],
        out_specs=[
            pl.BlockSpec((tm, tn), lambda i, j: (i, j)),
            pl.BlockSpec((tm, ns), lambda i, j: (i, 0)),
        ],
        out_shape=[
            jax.ShapeDtypeStruct((rows, n), F32),
            jax.ShapeDtypeStruct((rows, ns), F32),
        ],
        scratch_shapes=[pltpu.VMEM((tm, d), BF16)],
        compiler_params=_cparams("parallel", "arbitrary"),
        name="norm_proj",
    )(h, gain.reshape(1, d), w, w_small)


def _rope_table_kernel(c_ref, s1_ref, s2_ref):
    shape = c_ref.shape
    pos = lax.broadcasted_iota(jnp.int32, shape, 0).astype(F32)
    dim = lax.broadcasted_iota(jnp.int32, shape, 1) % SWA_HEAD_DIM
    half = SWA_HEAD_DIM // 2
    k = (dim % half).astype(F32)
    inv = jnp.exp(k * (-2.0 / SWA_HEAD_DIM * math.log(ROPE_THETA)))
    ang = pos * inv
    s = jnp.sin(ang)
    c_ref[...] = jnp.cos(ang)
    s1_ref[...] = jnp.where(dim < half, -s, 0.0)
    s2_ref[...] = jnp.where(dim >= half, s, 0.0)


def _rope_tables(seq):
    shp = jax.ShapeDtypeStruct((seq, LANES), F32)
    return pl.pallas_call(_rope_table_kernel, out_shape=(shp, shp, shp), name="rope_tables")()


def _swa_proj_kernel(h_ref, gain_ref, w_ref, c_ref, s1_ref, s2_ref, o_ref, *, n_rope):
    xn = _rms(h_ref[...], gain_ref[...]).astype(BF16)
    y = jnp.dot(xn, w_ref[...], preferred_element_type=F32)
    c, s1, s2 = c_ref[...], s1_ref[...], s2_ref[...]
    half = SWA_HEAD_DIM // 2
    for ch in range(y.shape[1] // LANES):
        yc = y[:, ch * LANES:(ch + 1) * LANES]
        if ch < n_rope:
            yc = (yc * c + pltpu.roll(yc, LANES - half, axis=1) * s1
                  + pltpu.roll(yc, half, axis=1) * s2)
        o_ref[:, ch * LANES:(ch + 1) * LANES] = yc.astype(o_ref.dtype)


def _swa_proj(h, gain, w, tables, seq):
    rows, d = h.shape
    n = w.shape[1]
    tm = _row_tile(seq)
    per_seq = seq // tm
    n_rope = (SWA_Q_HEADS + SWA_KV_HEADS) * SWA_HEAD_DIM // LANES
    tab = pl.BlockSpec((tm, LANES), lambda i: (i % per_seq, 0))
    return pl.pallas_call(
        functools.partial(_swa_proj_kernel, n_rope=n_rope),
        grid=(rows // tm,),
        in_specs=[
            pl.BlockSpec((tm, d), lambda i: (i, 0)),
            pl.BlockSpec((1, d), lambda i: (0, 0)),
            pl.BlockSpec((d, n), lambda i: (0, 0)),
            tab, tab, tab,
        ],
        out_specs=pl.BlockSpec((tm, n), lambda i: (i, 0)),
        out_shape=jax.ShapeDtypeStruct((rows, n), BF16),
        compiler_params=_cparams("parallel"),
        name="swa_proj",
    )(h, gain.reshape(1, d), w, *tables)


def _out_proj_kernel(a_ref, w_ref, h_ref, o_ref):
    o_ref[...] = h_ref[...] + jnp.dot(a_ref[...], w_ref[...], preferred_element_type=F32)


def _out_proj(a, w, h, tn=1024):
    rows, k = a.shape
    d = w.shape[1]
    tm = _row_tile(rows)
    return pl.pallas_call(
        _out_proj_kernel,
        grid=(rows // tm, d // tn),
        in_specs=[
            pl.BlockSpec((tm, k), lambda i, j: (i, 0)),
            pl.BlockSpec((k, tn), lambda i, j: (0, j)),
            pl.BlockSpec((tm, tn), lambda i, j: (i, j)),
        ],
        out_specs=pl.BlockSpec((tm, tn), lambda i, j: (i, j)),
        out_shape=jax.ShapeDtypeStruct((rows, d), F32),
        compiler_params=_cparams("parallel", "arbitrary"),
        name="out_proj",
    )(a, w, h)


def _swa_kernel(sink_ref, q_ref, kp_ref, kc_ref, km_ref, vp_ref, vc_ref, vm_ref, o_ref, *, seq):
    m = pl.program_id(1)
    w, hd, nm = SWA_WINDOW, SWA_HEAD_DIM, N_META
    nk = 2 * w + nm
    ii = lax.broadcasted_iota(jnp.int32, (w, nk), 0)
    jj = lax.broadcasted_iota(jnp.int32, (w, nk), 1)
    band = (jj > ii) & (jj <= ii + w) & ((m > 0) | (jj >= w))
    meta = (jj >= 2 * w) & (m * w + ii - (jj - 2 * w) >= w)
    mask = band | meta
    krow = lax.broadcasted_iota(jnp.int32, (nk, 1), 0)
    kabs = jnp.where(krow < 2 * w, (m - 1) * w + krow, krow - 2 * w)
    kvalid = (kabs >= 0) & (kabs < seq)
    k_all = jnp.concatenate([kp_ref[0], kc_ref[0], km_ref[0, :nm, :]], axis=0)
    v_all = jnp.concatenate([vp_ref[0], vc_ref[0], vm_ref[0, :nm, :]], axis=0)
    k_all = jnp.where(kvalid, k_all, jnp.zeros_like(k_all))
    v_all = jnp.where(kvalid, v_all, jnp.zeros_like(v_all))
    nh = SWA_Q_HEADS
    q = q_ref[0]
    q3 = jnp.stack([q[:, h * hd:(h + 1) * hd] for h in range(nh)], axis=0)
    k3 = jnp.stack([k_all[:, (h // SWA_GROUP) * hd:(h // SWA_GROUP + 1) * hd] for h in range(nh)], axis=0)
    v3 = jnp.stack([v_all[:, (h // SWA_GROUP) * hd:(h // SWA_GROUP + 1) * hd] for h in range(nh)], axis=0)
    s = jnp.einsum("hqd,hkd->hqk", q3, k3, preferred_element_type=F32)
    s = jnp.where(mask, s * (hd ** -0.5), -jnp.inf)
    sink = sink_ref[...]
    mx = jnp.maximum(jnp.max(s, axis=-1, keepdims=True), sink)
    p = jnp.exp(s - mx)
    den = jnp.sum(p, axis=-1, keepdims=True) + jnp.exp(sink - mx)
    o = jnp.einsum("hqk,hkd->hqd", p.astype(BF16), v3, preferred_element_type=F32) / den
    for h in range(0, nh, 2):
        o_ref[0, :, h * hd:(h + 2) * hd] = jnp.concatenate([o[h], o[h + 1]], axis=1).astype(o_ref.dtype)


def _swa_attention(qkv, sinks, batch, seq):
    n = qkv.shape[1]
    qkv = qkv.reshape(batch, seq, n)
    w = SWA_WINDOW
    nq = SWA_Q_HEADS * SWA_HEAD_DIM
    nkv = SWA_KV_HEADS * SWA_HEAD_DIM
    kcol = nq // nkv
    vcol = kcol + 1
    nblk = pl.cdiv(seq, w)

    def kv_spec(col, which):
        if which == "prev":
            return pl.BlockSpec((1, w, nkv), lambda b, m: (b, jnp.maximum(m - 1, 0), col))
        if which == "cur":
            return pl.BlockSpec((1, w, nkv), lambda b, m: (b, m, col))
        return pl.BlockSpec((1, w, nkv), lambda b, m: (b, 0, col))

    out = pl.pallas_call(
        functools.partial(_swa_kernel, seq=seq),
        grid=(batch, nblk),
        in_specs=[
            pl.BlockSpec((SWA_Q_HEADS, 1, 1), lambda b, m: (0, 0, 0)),
            pl.BlockSpec((1, w, nq), lambda b, m: (b, m, 0)),
            kv_spec(kcol, "prev"), kv_spec(kcol, "cur"), kv_spec(kcol, "meta"),
            kv_spec(vcol, "prev"), kv_spec(vcol, "cur"), kv_spec(vcol, "meta"),
        ],
        out_specs=pl.BlockSpec((1, w, nq), lambda b, m: (b, m, 0)),
        out_shape=jax.ShapeDtypeStruct((batch, seq, nq), BF16),
        compiler_params=_cparams("parallel", "parallel"),
        name="swa_attention",
    )(sinks.astype(F32).reshape(SWA_Q_HEADS, 1, 1), qkv, qkv, qkv, qkv, qkv, qkv, qkv)
    return out.reshape(batch * seq, nq)


def _bmm(a, b):
    return jnp.einsum("hmk,hkn->hmn", a.astype(BF16), b.astype(BF16), preferred_element_type=F32)


def _bmm_nt(a, b):
    return jnp.einsum("hmk,hnk->hmn", a.astype(BF16), b.astype(BF16), preferred_element_type=F32)


def _bmm_tn(a, b):
    return jnp.einsum("hkm,hkn->hmn", a.astype(BF16), b.astype(BF16), preferred_element_type=F32)


def _tri_inverse(a, ii, jj):
    c = a.shape[-1]
    eye = jnp.where(ii == jj, 1.0, 0.0)
    inv = eye - jnp.where((ii >> 1) == (jj >> 1), a, 0.0)
    lg = 1
    while (1 << lg) < c:
        same_pair = (ii >> (lg + 1)) == (jj >> (lg + 1))
        off = jnp.where(same_pair & ((ii >> lg) != (jj >> lg)), a, 0.0)
        inv = inv - _bmm(inv, _bmm(off, inv))
        lg += 1
    return inv


def _conv_silu(x_ref, tail_ref, w_ref):
    x = x_ref[0]
    c = x.shape[0]
    nt = tail_ref.shape[0]
    xe = jnp.concatenate([tail_ref[...], x], axis=0)
    w = w_ref[...]
    y = x * w[DN_CONV - 1:DN_CONV, :]
    for s in range(1, DN_CONV):
        y = y + pltpu.roll(xe, s, axis=0)[nt:, :] * w[DN_CONV - 1 - s:DN_CONV - s, :]
    tail_ref[...] = x[c - nt:, :]
    return _silu(y)


def _gdn_kernel(q_ref, k_ref, v_ref, z_ref, ba_ref, cwq_ref, cwk_ref, cwv_ref, alog_ref, dtb_ref,
                ogain_ref, o_ref, s_ref, tq_ref, tk_ref, tv_ref, *, nq, seq):
    c_idx = pl.program_id(2)
    nv = 2 * nq
    c, hd = DN_CHUNK, DN_HEAD

    @pl.when(c_idx == 0)
    def _():
        s_ref[...] = jnp.zeros_like(s_ref)
        tq_ref[...] = jnp.zeros_like(tq_ref)
        tk_ref[...] = jnp.zeros_like(tk_ref)
        tv_ref[...] = jnp.zeros_like(tv_ref)

    row = c_idx * c + lax.broadcasted_iota(jnp.int32, (c, 1), 0)
    valid = row < seq
    ii = lax.broadcasted_iota(jnp.int32, (c, c), 0)
    jj = lax.broadcasted_iota(jnp.int32, (c, c), 1)
    lower = ii >= jj

    qc = jnp.where(valid, _conv_silu(q_ref, tq_ref, cwq_ref), 0.0)
    kc = jnp.where(valid, _conv_silu(k_ref, tk_ref, cwk_ref), 0.0)
    vc = jnp.where(valid, _conv_silu(v_ref, tv_ref, cwv_ref), 0.0)

    ba = ba_ref[0]
    beta = jnp.where(valid, jax.nn.sigmoid(ba), 0.0)
    x = ba + dtb_ref[0]
    softplus = jnp.maximum(x, 0.0) + jnp.log1p(jnp.exp(-jnp.abs(x)))
    g = jnp.where(valid, -jnp.exp(alog_ref[0]) * softplus, 0.0)
    rr = lax.broadcasted_iota(jnp.int32, g.shape, 0)
    gc = g
    s = 1
    while s < c:
        gc = gc + jnp.where(rr >= s, pltpu.roll(gc, s, axis=0), 0.0)
        s *= 2
    gct = gc.T
    glast = gc[c - 1:c, :]

    def heads(x, n):
        return jnp.stack([x[:, h * hd:(h + 1) * hd] for h in range(n)], axis=0)

    def per_v_head(x):
        return jnp.stack([x[l // 2] for l in range(nv)], axis=0)

    qh = heads(qc, nq)
    kh = heads(kc, nq)
    qh = qh * lax.rsqrt(jnp.sum(qh * qh, axis=-1, keepdims=True) + L2_EPS) * (hd ** -0.5)
    kh = kh * lax.rsqrt(jnp.sum(kh * kh, axis=-1, keepdims=True) + L2_EPS)
    kk = per_v_head(_bmm_nt(kh, kh))
    qk = per_v_head(_bmm_nt(qh, kh))
    qv = per_v_head(qh)
    kv = per_v_head(kh)
    vh = heads(vc, nv)
    zh = heads(z_ref[0], nv)
    bi = jnp.stack([beta[:, l:l + 1] for l in range(nv)], axis=0)
    gi = jnp.stack([gc[:, nv + l:nv + l + 1] for l in range(nv)], axis=0)
    gj = jnp.stack([gct[nv + l:nv + l + 1, :] for l in range(nv)], axis=0)
    gl = jnp.stack([glast[:, nv + l:nv + l + 1] for l in range(nv)], axis=0)

    decay = jnp.where(lower, jnp.exp(jnp.where(lower, gi - gj, 0.0)), 0.0)
    a = jnp.where(ii > jj, bi * kk * decay, 0.0)
    t = _tri_inverse(a, ii, jj)
    egi = jnp.exp(gi)
    kb = kv * bi
    sol = _bmm(t, jnp.concatenate([vh * bi, kb * egi], axis=-1))
    u, wmat = sol[:, :, :hd], sol[:, :, hd:]
    state = s_ref[...]
    v_new = u - _bmm(wmat, state)
    out = _bmm(qv * egi, state) + _bmm(qk * decay, v_new)
    s_ref[...] = state * jnp.exp(gl) + _bmm_tn(kv * jnp.exp(gl - gi), v_new)
    o = _rms(out, ogain_ref[...]) * _silu(zh)
    for l in range(nv):
        o_ref[0, :, l * hd:(l + 1) * hd] = o[l].astype(o_ref.dtype)


def _gdn_mix(proj, ba, conv_w, alog_g, dtb_g, o_gain, batch, seq):
    nq = DN_QK_PER_STEP
    nv = 2 * nq
    groups = DN_QK_HEADS // nq
    proj = proj.reshape(batch, seq, proj.shape[1])
    ba = ba.reshape(batch, seq, ba.shape[1])
    c = DN_CHUNK
    wq, wv = nq * DN_HEAD, nv * DN_HEAD
    k_off = DN_KEY_DIM // wq
    v_off = 2 * DN_KEY_DIM // wv
    z_off = (2 * DN_KEY_DIM + DN_VAL_DIM) // wv

    def col(width, off):
        return pl.BlockSpec((1, c, width), lambda b, g, t: (b, t, off + g))

    def cw(width, off):
        return pl.BlockSpec((DN_CONV, width), lambda b, g, t: (0, off + g))

    gate = pl.BlockSpec((1, 1, LANES), lambda b, g, t: (g, 0, 0))
    out = pl.pallas_call(
        functools.partial(_gdn_kernel, nq=nq, seq=seq),
        grid=(batch, groups, pl.cdiv(seq, c)),
        in_specs=[
            col(wq, 0), col(wq, k_off), col(wv, v_off), col(wv, z_off), col(LANES, 0),
            cw(wq, 0), cw(wq, k_off), cw(wv, v_off),
            gate, gate,
            pl.BlockSpec((1, DN_HEAD), lambda b, g, t: (0, 0)),
        ],
        out_specs=pl.BlockSpec((1, c, wv), lambda b, g, t: (b, t, g)),
        out_shape=jax.ShapeDtypeStruct((batch, seq, DN_VAL_DIM), BF16),
        scratch_shapes=[
            pltpu.VMEM((nv, DN_HEAD, DN_HEAD), F32),
            pltpu.VMEM((CONV_HALO, wq), F32),
            pltpu.VMEM((CONV_HALO, wq), F32),
            pltpu.VMEM((CONV_HALO, wv), F32),
        ],
        compiler_params=_cparams("parallel", "parallel", "arbitrary"),
        name="gdn_mix",
    )(proj, proj, proj, proj, ba, conv_w, conv_w, conv_w, alog_g, dtb_g,
      o_gain.reshape(1, DN_HEAD))
    return out.reshape(batch * seq, DN_VAL_DIM)


def _prep_gdn_weights(w_in_all, layer, a_log, dt_bias):
    nq = DN_QK_PER_STEP
    nv = 2 * nq
    groups = DN_QK_HEADS // nq
    d = w_in_all.shape[1]
    main = 2 * DN_KEY_DIM + 2 * DN_VAL_DIM
    gates = w_in_all[layer, :, main:]
    w_b = gates[:, :DN_V_HEADS].reshape(d, groups, nv)
    w_a = gates[:, DN_V_HEADS:].reshape(d, groups, nv)
    pad = jnp.zeros((d, groups, LANES - 2 * nv), gates.dtype)
    w_ba = jnp.concatenate([w_b, w_a, pad], axis=2).reshape(d, groups * LANES).astype(BF16)
    w_main = _cast_bf16(w_in_all, layer, 0, main)

    def gate_rows(p):
        p = p.astype(F32).reshape(groups, nv)
        z = jnp.zeros((groups, nv), F32)
        zp = jnp.zeros((groups, LANES - 2 * nv), F32)
        return jnp.concatenate([z, p, zp], axis=1).reshape(groups, 1, LANES)

    return w_main, w_ba, gate_rows(a_log), gate_rows(dt_bias)


def _final_norm_kernel(h_ref, gain_ref, o_ref):
    o_ref[...] = _rms(h_ref[0], gain_ref[...])


def _final_norm(h, gain, batch, seq):
    d = h.shape[1]
    h = h.reshape(batch, seq, d)
    n_out = seq - N_META
    tr = 256 if n_out % 256 == 0 else n_out
    return pl.pallas_call(
        _final_norm_kernel,
        grid=(batch, n_out // tr),
        in_specs=[
            pl.BlockSpec((pl.Element(1), pl.Element(tr), pl.Element(d)),
                         lambda b, r: (b, pl.multiple_of(N_META + r * tr, 8), 0)),
            pl.BlockSpec((1, d), lambda b, r: (0, 0)),
        ],
        out_specs=pl.BlockSpec((None, tr, d), lambda b, r: (b, r, 0)),
        out_shape=jax.ShapeDtypeStruct((batch, n_out, d), F32),
        compiler_params=_cparams("parallel", "parallel"),
        name="final_norm",
    )(h, gain.reshape(1, d))


def kernel(x, meta_tokens, ffn_pre_norm, ffn_pre_w_gu, ffn_pre_w_down, mix_norm, ffn_post_norm,
           ffn_post_w_gu, ffn_post_w_down, dn_w_in, dn_conv_w, dn_a_log, dn_dt_bias, dn_out_norm,
           dn_w_out, swa_w_qkv, swa_sinks, swa_w_out, final_norm):
    batch, seq_in, d = x.shape
    seq = seq_in + N_META
    depth = ffn_pre_norm.shape[0]
    meta = jnp.broadcast_to(meta_tokens[None].astype(x.dtype), (batch, N_META, d))
    h = jnp.concatenate([meta, x], axis=1).reshape(batch * seq, d)
    tables = None
    for i in range(depth):
        j = i // 2
        h = _ffn(h, ffn_pre_norm[i], *_prep_ffn_weights(ffn_pre_w_gu, ffn_pre_w_down, i))
        if i % 2 == 0:
            w_main, w_ba, alog_g, dtb_g = _prep_gdn_weights(dn_w_in, j, dn_a_log[j], dn_dt_bias[j])
            proj, ba = _norm_proj(h, mix_norm[i], w_main, w_ba)
            o = _gdn_mix(proj, ba, dn_conv_w[j], alog_g, dtb_g, dn_out_norm[j], batch, seq)
            h = _out_proj(o, _cast_bf16(dn_w_out, j, 0, d), h)
        else:
            if tables is None:
                tables = _rope_tables(seq)
            w_qkv = _cast_bf16(swa_w_qkv, j, 0, swa_w_qkv.shape[2])
            qkv = _swa_proj(h, mix_norm[i], w_qkv, tables, seq)
            o = _swa_attention(qkv, swa_sinks[j], batch, seq)
            h = _out_proj(o, _cast_bf16(swa_w_out, j, 0, d), h)
        h = _ffn(h, ffn_post_norm[i], *_prep_ffn_weights(ffn_post_w_gu, ffn_post_w_down, i))
    return _final_norm(h, final_norm, batch, seq)
```

```python
import functools
import math

import jax
import jax.numpy as jnp
from jax import lax
from jax.experimental import pallas as pl
from jax.experimental.pallas import tpu as pltpu

F32 = jnp.float32
BF16 = jnp.bfloat16

N_META = 16
NORM_EPS = 1e-6
L2_EPS = 1e-6
FFN_RES = 0.5
DN_QK_HEADS = 16
DN_V_HEADS = 32
DN_HEAD = 128
DN_CONV = 4
DN_CHUNK = 64
DN_KEY_DIM = DN_QK_HEADS * DN_HEAD
DN_VAL_DIM = DN_V_HEADS * DN_HEAD
SWA_Q_HEADS = 32
SWA_KV_HEADS = 4
SWA_HEAD_DIM = 64
SWA_GROUP = SWA_Q_HEADS // SWA_KV_HEADS
SWA_WINDOW = 128
ROPE_THETA = 10000.0
LOG2_E = math.log2(math.e)

LANES = 128
SWA_KEYS = 3 * LANES
assert SWA_KEYS > 2 * SWA_WINDOW + N_META
VMEM_LIMIT = 56 * 1024 * 1024
ROW_TILE_MAX = 688
FF_TILE = 512
PROJ_TILE = 2048
CAST_TILE_BYTES = 6 * 1024 * 1024
CONV_HALO = 8
DN_QK_PER_STEP = 16


def _cparams(*sem):
    return pltpu.CompilerParams(dimension_semantics=sem, vmem_limit_bytes=VMEM_LIMIT)


def _row_tile(rows):
    best = None
    for t in range(16, min(rows, ROW_TILE_MAX) + 1, 16):
        if rows % t == 0:
            best = t
    assert best is not None, rows
    return best


def _rms(x, gain):
    return x * lax.rsqrt(jnp.mean(x * x, axis=-1, keepdims=True) + NORM_EPS) * gain


def _silu(x):
    return x * jax.nn.sigmoid(x)


def _mm(a, b):
    return jnp.dot(a.astype(BF16), b.astype(BF16), preferred_element_type=F32)


def _ffn_kernel(h_ref, gain_ref, wg_ref, wu_ref, wd_ref, o_ref, xn_ref, *, d_ff):
    j = pl.program_id(1)

    @pl.when(j == 0)
    def _():
        h = h_ref[...]
        xn_ref[...] = _rms(h, gain_ref[...]).astype(BF16)
        o_ref[...] = h

    xn = xn_ref[...]
    g = jnp.dot(xn, wg_ref[...], preferred_element_type=F32)
    u = jnp.dot(xn, wu_ref[...], preferred_element_type=F32)
    a = _silu(g) * u
    wd = wd_ref[...]
    if d_ff % FF_TILE:
        col = j * FF_TILE + lax.broadcasted_iota(jnp.int32, (1, FF_TILE), 1)
        a = jnp.where(col < d_ff, a, 0.0)
        row = j * FF_TILE + lax.broadcasted_iota(jnp.int32, (FF_TILE, 1), 0)
        wd = jnp.where(row < d_ff, wd, jnp.zeros_like(wd))
    o_ref[...] += FFN_RES * jnp.dot(a.astype(BF16), wd, preferred_element_type=F32)


def _ffn(h, gain, wg, wu, wd):
    rows, d = h.shape
    d_ff = wd.shape[0]
    tm = _row_tile(rows)
    return pl.pallas_call(
        functools.partial(_ffn_kernel, d_ff=d_ff),
        grid=(rows // tm, pl.cdiv(d_ff, FF_TILE)),
        in_specs=[
            pl.BlockSpec((tm, d), lambda i, j: (i, 0)),
            pl.BlockSpec((1, d), lambda i, j: (0, 0)),
            pl.BlockSpec((d, FF_TILE), lambda i, j: (0, j)),
            pl.BlockSpec((d, FF_TILE), lambda i, j: (0, j)),
            pl.BlockSpec((FF_TILE, d), lambda i, j: (j, 0)),
        ],
        out_specs=pl.BlockSpec((tm, d), lambda i, j: (i, 0)),
        out_shape=jax.ShapeDtypeStruct((rows, d), F32),
        scratch_shapes=[pltpu.VMEM((tm, d), BF16)],
        compiler_params=_cparams("parallel", "arbitrary"),
        name="ffn",
    )(h, gain.reshape(1, d), wg, wu, wd)


def _cast_kernel(w_ref, o_ref, *, valid_cols):
    w = w_ref[...]
    if valid_cols < w.shape[1]:
        col = lax.broadcasted_iota(jnp.int32, (1, w.shape[1]), 1)
        w = jnp.where(col < valid_cols, w, 0.0)
    o_ref[...] = w.astype(o_ref.dtype)


def _cast_bf16(w, layer, col_block, n_cols, valid_cols=None):
    rows = w.shape[1]
    valid_cols = n_cols if valid_cols is None else valid_cols
    tr = None
    for t in range(16, rows + 1, 16):
        if rows % t == 0 and t * n_cols * 4 <= CAST_TILE_BYTES:
            tr = t
    assert tr is not None, (rows, n_cols)
    return pl.pallas_call(
        functools.partial(_cast_kernel, valid_cols=valid_cols),
        grid=(rows // tr,),
        in_specs=[pl.BlockSpec((None, tr, n_cols), lambda r: (layer, r, col_block))],
        out_specs=pl.BlockSpec((tr, n_cols), lambda r: (r, 0)),
        out_shape=jax.ShapeDtypeStruct((rows, n_cols), BF16),
        compiler_params=_cparams("parallel"),
        name="cast_bf16",
    )(w)


def _prep_ffn_weights(w_gu, w_down, layer):
    d_ff = w_down.shape[1]
    return (_cast_bf16(w_gu, layer, 0, d_ff), _cast_bf16(w_gu, layer, 1, d_ff),
            _cast_bf16(w_down, layer, 0, w_down.shape[2]))


def _proj_kernel(h_ref, gain_ref, w_ref, ws_ref, o_ref, os_ref, xn_ref):
    @pl.when(pl.program_id(1) == 0)
    def _():
        xn = _rms(h_ref[...], gain_ref[...]).astype(BF16)
        xn_ref[...] = xn
        os_ref[...] = jnp.dot(xn, ws_ref[...], preferred_element_type=F32)

    o_ref[...] = jnp.dot(xn_ref[...], w_ref[...], preferred_element_type=F32)


def _norm_proj(h, gain, w, w_small):
    rows, d = h.shape
    n, ns = w.shape[1], w_small.shape[1]
    tm = _row_tile(rows)
    tn = PROJ_TILE
    assert n % tn == 0
    return pl.pallas_call(
        _proj_kernel,
        grid=(rows // tm, n // tn),
        in_specs=[
            pl.BlockSpec((tm, d), lambda i, j: (i, 0)),
            pl.BlockSpec((1, d), lambda i, j: (0, 0)),
            pl.BlockSpec((d, tn), lambda i, j: (0, j)),
            pl.BlockSpec((d, ns), lambda i, j: (0, 0)),
        ],
        out_specs=[
            pl.BlockSpec((tm, tn), lambda i, j: (i, j)),
            pl.BlockSpec((tm, ns), lambda i, j: (i, 0)),
        ],
        out_shape=[
            jax.ShapeDtypeStruct((rows, n), F32),
            jax.ShapeDtypeStruct((rows, ns), F32),
        ],
        scratch_shapes=[pltpu.VMEM((tm, d), BF16)],
        compiler_params=_cparams("parallel", "arbitrary"),
        name="norm_proj",
    )(h, gain.reshape(1, d), w, w_small)


def _rope_table_kernel(c_ref, s1_ref, s2_ref):
    shape = c_ref.shape
    pos = lax.broadcasted_iota(jnp.int32, shape, 0).astype(F32)
    dim = lax.broadcasted_iota(jnp.int32, shape, 1) % SWA_HEAD_DIM
    half = SWA_HEAD_DIM // 2
    k = (dim % half).astype(F32)
    inv = jnp.exp(k * (-2.0 / SWA_HEAD_DIM * math.log(ROPE_THETA)))
    ang = pos * inv
    s = jnp.sin(ang)
    c_ref[...] = jnp.cos(ang)
    s1_ref[...] = jnp.where(dim < half, -s, 0.0)
    s2_ref[...] = jnp.where(dim >= half, s, 0.0)


def _rope_tables(seq):
    shp = jax.ShapeDtypeStruct((seq, LANES), F32)
    return pl.pallas_call(_rope_table_kernel, out_shape=(shp, shp, shp), name="rope_tables")()


def _swa_proj_kernel(h_ref, gain_ref, w_ref, c_ref, s1_ref, s2_ref, o_ref, *, n_rope):
    xn = _rms(h_ref[...], gain_ref[...]).astype(BF16)
    y = jnp.dot(xn, w_ref[...], preferred_element_type=F32)
    c, s1, s2 = c_ref[...], s1_ref[...], s2_ref[...]
    half = SWA_HEAD_DIM // 2
    for ch in range(y.shape[1] // LANES):
        yc = y[:, ch * LANES:(ch + 1) * LANES]
        if ch < n_rope:
            yc = (yc * c + pltpu.roll(yc, LANES - half, axis=1) * s1
                  + pltpu.roll(yc, half, axis=1) * s2)
        o_ref[:, ch * LANES:(ch + 1) * LANES] = yc.astype(o_ref.dtype)


def _swa_proj(h, gain, w, tables, seq):
    rows, d = h.shape
    n = w.shape[1]
    tm = _row_tile(seq)
    per_seq = seq // tm
    n_rope = (SWA_Q_HEADS + SWA_KV_HEADS) * SWA_HEAD_DIM // LANES
    tab = pl.BlockSpec((tm, LANES), lambda i: (i % per_seq, 0))
    return pl.pallas_call(
        functools.partial(_swa_proj_kernel, n_rope=n_rope),
        grid=(rows // tm,),
        in_specs=[
            pl.BlockSpec((tm, d), lambda i: (i, 0)),
            pl.BlockSpec((1, d), lambda i: (0, 0)),
            pl.BlockSpec((d, n), lambda i: (0, 0)),
            tab, tab, tab,
        ],
        out_specs=pl.BlockSpec((tm, n), lambda i: (i, 0)),
        out_shape=jax.ShapeDtypeStruct((rows, n), BF16),
        compiler_params=_cparams("parallel"),
        name="swa_proj",
    )(h, gain.reshape(1, d), w, *tables)


def _out_proj_kernel(a_ref, w_ref, h_ref, o_ref):
    o_ref[...] = h_ref[...] + jnp.dot(a_ref[...], w_ref[...], preferred_element_type=F32)


def _out_proj(a, w, h, tn=1024):
    rows, k = a.shape
    d = w.shape[1]
    tm = _row_tile(rows)
    return pl.pallas_call(
        _out_proj_kernel,
        grid=(rows // tm, d // tn),
        in_specs=[
            pl.BlockSpec((tm, k), lambda i, j: (i, 0)),
            pl.BlockSpec((k, tn), lambda i, j: (0, j)),
            pl.BlockSpec((tm, tn), lambda i, j: (i, j)),
        ],
        out_specs=pl.BlockSpec((tm, tn), lambda i, j: (i, j)),
        out_shape=jax.ShapeDtypeStruct((rows, d), F32),
        compiler_params=_cparams("parallel", "arbitrary"),
        name="out_proj",
    )(a, w, h)


def _swa_kernel(sink_ref, q_ref, kp_ref, kc_ref, km_ref, vp_ref, vc_ref, vm_ref, o_ref, *, seq):
    m = pl.program_id(1)
    w, hd, nm = SWA_WINDOW, SWA_HEAD_DIM, N_META
    nk = SWA_KEYS
    n_real = 2 * w + nm
    ii = lax.broadcasted_iota(jnp.int32, (w, nk), 0)
    jj = lax.broadcasted_iota(jnp.int32, (w, nk), 1)
    band = (jj > ii) & (jj <= ii + w) & ((m > 0) | (jj >= w))
    meta = (jj >= 2 * w) & (jj < n_real) & (m * w + ii - (jj - 2 * w) >= w)
    mask = band | meta
    is_sink = lax.broadcasted_iota(jnp.int32, (1, nk), 1) == n_real
    krow = lax.broadcasted_iota(jnp.int32, (n_real, 1), 0)
    kabs = jnp.where(krow < 2 * w, (m - 1) * w + krow, krow - 2 * w)
    kvalid = (kabs >= 0) & (kabs < seq)
    k_all = jnp.concatenate([kp_ref[0], kc_ref[0], km_ref[0, :nm, :]], axis=0)
    v_all = jnp.concatenate([vp_ref[0], vc_ref[0], vm_ref[0, :nm, :]], axis=0)
    fill = jnp.zeros((nk - n_real, k_all.shape[1]), k_all.dtype)
    k_all = jnp.concatenate([jnp.where(kvalid, k_all, jnp.zeros_like(k_all)), fill], axis=0)
    v_all = jnp.concatenate([jnp.where(kvalid, v_all, jnp.zeros_like(v_all)), fill], axis=0)
    ones = jnp.ones((nk, hd), k_all.dtype)
    zeros = jnp.zeros((nk, hd), k_all.dtype)
    q = q_ref[0]
    npair = SWA_GROUP // 2

    def block_diag(x, extra_lo, extra_hi):
        lo = jnp.concatenate([x, jnp.zeros_like(x)] + extra_lo, axis=1)
        hi = jnp.concatenate([jnp.zeros_like(x), x] + extra_hi, axis=1)
        return jnp.concatenate([lo, hi], axis=0)

    scores = []
    for kv in range(SWA_KV_HEADS):
        k_bd = block_diag(k_all[:, kv * hd:(kv + 1) * hd], [], [])
        q3 = jnp.stack([q[:, (kv * npair + j) * 2 * hd:(kv * npair + j + 1) * 2 * hd]
                        for j in range(npair)], axis=0)
        scores.append(jnp.einsum("pqd,pkd->pqk", q3, jnp.stack([k_bd] * npair, axis=0),
                                 preferred_element_type=F32))
    for kv in range(SWA_KV_HEADS):
        v_bd = block_diag(v_all[:, kv * hd:(kv + 1) * hd], [ones, zeros], [zeros, ones])
        probs = []
        for half in range(2):
            s = scores[kv][:, :, half * nk:(half + 1) * nk]
            sink = jnp.stack([sink_ref[kv * SWA_GROUP + 2 * j + half] for j in range(npair)], axis=0)
            s = jnp.where(mask, s * (hd ** -0.5 * LOG2_E), jnp.where(is_sink, sink * LOG2_E, -jnp.inf))
            probs.append(jnp.exp2(s - jnp.max(s, axis=-1, keepdims=True)).astype(BF16))
        p = jnp.concatenate(probs, axis=-1)
        o = jnp.einsum("pqk,pkd->pqd", p, jnp.stack([v_bd] * npair, axis=0),
                       preferred_element_type=F32)
        o = o[:, :, :2 * hd] / o[:, :, 2 * hd:]
        for j in range(npair):
            c0 = (kv * npair + j) * 2 * hd
            o_ref[0, :, c0:c0 + 2 * hd] = o[j].astype(o_ref.dtype)


def _swa_attention(qkv, sinks, batch, seq):
    n = qkv.shape[1]
    qkv = qkv.reshape(batch, seq, n)
    w = SWA_WINDOW
    nq = SWA_Q_HEADS * SWA_HEAD_DIM
    nkv = SWA_KV_HEADS * SWA_HEAD_DIM
    kcol = nq // nkv
    vcol = kcol + 1
    nblk = pl.cdiv(seq, w)

    def kv_spec(col, which):
        if which == "prev":
            return pl.BlockSpec((1, w, nkv), lambda b, m: (b, jnp.maximum(m - 1, 0), col))
        if which == "cur":
            return pl.BlockSpec((1, w, nkv), lambda b, m: (b, m, col))
        return pl.BlockSpec((1, w, nkv), lambda b, m: (b, 0, col))

    out = pl.pallas_call(
        functools.partial(_swa_kernel, seq=seq),
        grid=(batch, nblk),
        in_specs=[
            pl.BlockSpec((SWA_Q_HEADS, 1, 1), lambda b, m: (0, 0, 0)),
            pl.BlockSpec((1, w, nq), lambda b, m: (b, m, 0)),
            kv_spec(kcol, "prev"), kv_spec(kcol, "cur"), kv_spec(kcol, "meta"),
            kv_spec(vcol, "prev"), kv_spec(vcol, "cur"), kv_spec(vcol, "meta"),
        ],
        out_specs=pl.BlockSpec((1, w, nq), lambda b, m: (b, m, 0)),
        out_shape=jax.ShapeDtypeStruct((batch, seq, nq), BF16),
        compiler_params=_cparams("parallel", "parallel"),
        name="swa_attention",
    )(sinks.astype(F32).reshape(SWA_Q_HEADS, 1, 1), qkv, qkv, qkv, qkv, qkv, qkv, qkv)
    return out.reshape(batch * seq, nq)


def _bmm(a, b):
    return jnp.einsum("hmk,hkn->hmn", a.astype(BF16), b.astype(BF16), preferred_element_type=F32)


def _bmm_nt(a, b):
    return jnp.einsum("hmk,hnk->hmn", a.astype(BF16), b.astype(BF16), preferred_element_type=F32)


def _bmm_tn(a, b):
    return jnp.einsum("hkm,hkn->hmn", a.astype(BF16), b.astype(BF16), preferred_element_type=F32)


def _tri_inverse(a, ii, jj):
    c = a.shape[-1]
    eye = jnp.where(ii == jj, 1.0, 0.0)
    inv = eye - jnp.where((ii >> 1) == (jj >> 1), a, 0.0)
    lg = 1
    while (1 << lg) < c:
        same_pair = (ii >> (lg + 1)) == (jj >> (lg + 1))
        off = jnp.where(same_pair & ((ii >> lg) != (jj >> lg)), a, 0.0)
        inv = inv - _bmm(inv, _bmm(off, inv))
        lg += 1
    return inv


def _conv_silu(x_ref, tail_ref, w_ref):
    x = x_ref[0]
    c = x.shape[0]
    nt = tail_ref.shape[0]
    xe = jnp.concatenate([tail_ref[...], x], axis=0)
    w = w_ref[...]
    y = x * w[DN_CONV - 1:DN_CONV, :]
    for s in range(1, DN_CONV):
        y = y + pltpu.roll(xe, s, axis=0)[nt:, :] * w[DN_CONV - 1 - s:DN_CONV - s, :]
    tail_ref[...] = x[c - nt:, :]
    return _silu(y)


def _gdn_kernel(q_ref, k_ref, v_ref, z_ref, ba_ref, cwq_ref, cwk_ref, cwv_ref, alog_ref, dtb_ref,
                ogain_ref, o_ref, s_ref, tq_ref, tk_ref, tv_ref, *, nq, seq):
    c_idx = pl.program_id(2)
    nv = 2 * nq
    c, hd = DN_CHUNK, DN_HEAD

    @pl.when(c_idx == 0)
    def _():
        s_ref[...] = jnp.zeros_like(s_ref)
        tq_ref[...] = jnp.zeros_like(tq_ref)
        tk_ref[...] = jnp.zeros_like(tk_ref)
        tv_ref[...] = jnp.zeros_like(tv_ref)

    row = c_idx * c + lax.broadcasted_iota(jnp.int32, (c, 1), 0)
    valid = row < seq
    ii = lax.broadcasted_iota(jnp.int32, (c, c), 0)
    jj = lax.broadcasted_iota(jnp.int32, (c, c), 1)
    lower = ii >= jj

    qc = jnp.where(valid, _conv_silu(q_ref, tq_ref, cwq_ref), 0.0)
    kc = jnp.where(valid, _conv_silu(k_ref, tk_ref, cwk_ref), 0.0)
    vc = jnp.where(valid, _conv_silu(v_ref, tv_ref, cwv_ref), 0.0)

    ba = ba_ref[0]
    beta = jnp.where(valid, jax.nn.sigmoid(ba), 0.0)
    x = ba + dtb_ref[0]
    softplus = jnp.maximum(x, 0.0) + jnp.log1p(jnp.exp(-jnp.abs(x)))
    g = jnp.where(valid, -jnp.exp(alog_ref[0]) * softplus, 0.0)
    rr = lax.broadcasted_iota(jnp.int32, g.shape, 0)
    gc = g
    s = 1
    while s < c:
        gc = gc + jnp.where(rr >= s, pltpu.roll(gc, s, axis=0), 0.0)
        s *= 2
    gct = gc.T
    glast = gc[c - 1:c, :]

    def heads(x, n):
        return jnp.stack([x[:, h * hd:(h + 1) * hd] for h in range(n)], axis=0)

    def per_v_head(x):
        return jnp.stack([x[l // 2] for l in range(nv)], axis=0)

    qh = heads(qc, nq)
    kh = heads(kc, nq)
    qh = qh * lax.rsqrt(jnp.sum(qh * qh, axis=-1, keepdims=True) + L2_EPS) * (hd ** -0.5)
    kh = kh * lax.rsqrt(jnp.sum(kh * kh, axis=-1, keepdims=True) + L2_EPS)
    kk = per_v_head(_bmm_nt(kh, kh))
    qk = per_v_head(_bmm_nt(qh, kh))
    qv = per_v_head(qh)
    kv = per_v_head(kh)
    vh = heads(vc, nv)
    zh = heads(z_ref[0], nv)
    bi = jnp.stack([beta[:, l:l + 1] for l in range(nv)], axis=0)
    gi = jnp.stack([gc[:, nv + l:nv + l + 1] for l in range(nv)], axis=0)
    gj = jnp.stack([gct[nv + l:nv + l + 1, :] for l in range(nv)], axis=0)
    gl = jnp.stack([glast[:, nv + l:nv + l + 1] for l in range(nv)], axis=0)

    decay = jnp.where(lower, jnp.exp(jnp.where(lower, gi - gj, 0.0)), 0.0)
    a = jnp.where(ii > jj, bi * kk * decay, 0.0)
    t = _tri_inverse(a, ii, jj)
    egi = jnp.exp(gi)
    kb = kv * bi
    sol = _bmm(t, jnp.concatenate([vh * bi, kb * egi], axis=-1))
    u, wmat = sol[:, :, :hd], sol[:, :, hd:]
    state = s_ref[...]
    v_new = u - _bmm(wmat, state)
    out = _bmm(qv * egi, state) + _bmm(qk * decay, v_new)
    s_ref[...] = state * jnp.exp(gl) + _bmm_tn(kv * jnp.exp(gl - gi), v_new)
    o = _rms(out, ogain_ref[...]) * _silu(zh)
    for l in range(nv):
        o_ref[0, :, l * hd:(l + 1) * hd] = o[l].astype(o_ref.dtype)


def _gdn_mix(proj, ba, conv_w, alog_g, dtb_g, o_gain, batch, seq):
    nq = DN_QK_PER_STEP
    nv = 2 * nq
    groups = DN_QK_HEADS // nq
    proj = proj.reshape(batch, seq, proj.shape[1])
    ba = ba.reshape(batch, seq, ba.shape[1])
    c = DN_CHUNK
    wq, wv = nq * DN_HEAD, nv * DN_HEAD
    k_off = DN_KEY_DIM // wq
    v_off = 2 * DN_KEY_DIM // wv
    z_off = (2 * DN_KEY_DIM + DN_VAL_DIM) // wv

    def col(width, off):
        return pl.BlockSpec((1, c, width), lambda b, g, t: (b, t, off + g))

    def cw(width, off):
        return pl.BlockSpec((DN_CONV, width), lambda b, g, t: (0, off + g))

    gate = pl.BlockSpec((1, 1, LANES), lambda b, g, t: (g, 0, 0))
    out = pl.pallas_call(
        functools.partial(_gdn_kernel, nq=nq, seq=seq),
        grid=(batch, groups, pl.cdiv(seq, c)),
        in_specs=[
            col(wq, 0), col(wq, k_off), col(wv, v_off), col(wv, z_off), col(LANES, 0),
            cw(wq, 0), cw(wq, k_off), cw(wv, v_off),
            gate, gate,
            pl.BlockSpec((1, DN_HEAD), lambda b, g, t: (0, 0)),
        ],
        out_specs=pl.BlockSpec((1, c, wv), lambda b, g, t: (b, t, g)),
        out_shape=jax.ShapeDtypeStruct((batch, seq, DN_VAL_DIM), BF16),
        scratch_shapes=[
            pltpu.VMEM((nv, DN_HEAD, DN_HEAD), F32),
            pltpu.VMEM((CONV_HALO, wq), F32),
            pltpu.VMEM((CONV_HALO, wq), F32),
            pltpu.VMEM((CONV_HALO, wv), F32),
        ],
        compiler_params=_cparams("parallel", "parallel", "arbitrary"),
        name="gdn_mix",
    )(proj, proj, proj, proj, ba, conv_w, conv_w, conv_w, alog_g, dtb_g,
      o_gain.reshape(1, DN_HEAD))
    return out.reshape(batch * seq, DN_VAL_DIM)


def _prep_gdn_weights(w_in_all, layer, a_log, dt_bias):
    nq = DN_QK_PER_STEP
    nv = 2 * nq
    groups = DN_QK_HEADS // nq
    main = 2 * DN_KEY_DIM + 2 * DN_VAL_DIM
    assert groups == 1 and main % LANES == 0 and 2 * nv <= LANES
    w_ba = _cast_bf16(w_in_all, layer, main // LANES, LANES, valid_cols=2 * nv)
    w_main = _cast_bf16(w_in_all, layer, 0, main)

    def gate_rows(p):
        p = p.astype(F32).reshape(groups, nv)
        z = jnp.zeros((groups, nv), F32)
        zp = jnp.zeros((groups, LANES - 2 * nv), F32)
        return jnp.concatenate([z, p, zp], axis=1).reshape(groups, 1, LANES)

    return w_main, w_ba, gate_rows(a_log), gate_rows(dt_bias)


def _final_norm_kernel(h_ref, gain_ref, o_ref):
    o_ref[...] = _rms(h_ref[0], gain_ref[...])


def _final_norm(h, gain, batch, seq):
    d = h.shape[1]
    h = h.reshape(batch, seq, d)
    n_out = seq - N_META
    tr = 256 if n_out % 256 == 0 else n_out
    return pl.pallas_call(
        _final_norm_kernel,
        grid=(batch, n_out // tr),
        in_specs=[
            pl.BlockSpec((pl.Element(1), pl.Element(tr), pl.Element(d)),
                         lambda b, r: (b, pl.multiple_of(N_META + r * tr, 8), 0)),
            pl.BlockSpec((1, d), lambda b, r: (0, 0)),
        ],
        out_specs=pl.BlockSpec((None, tr, d), lambda b, r: (b, r, 0)),
        out_shape=jax.ShapeDtypeStruct((batch, n_out, d), F32),
        compiler_params=_cparams("parallel", "parallel"),
        name="final_norm",
    )(h, gain.reshape(1, d))


def kernel(x, meta_tokens, ffn_pre_norm, ffn_pre_w_gu, ffn_pre_w_down, mix_norm, ffn_post_norm,
           ffn_post_w_gu, ffn_post_w_down, dn_w_in, dn_conv_w, dn_a_log, dn_dt_bias, dn_out_norm,
           dn_w_out, swa_w_qkv, swa_sinks, swa_w_out, final_norm):
    batch, seq_in, d = x.shape
    seq = seq_in + N_META
    depth = ffn_pre_norm.shape[0]
    meta = jnp.broadcast_to(meta_tokens[None].astype(x.dtype), (batch, N_META, d))
    h = jnp.concatenate([meta, x], axis=1).reshape(batch * seq, d)
    tables = None
    for i in range(depth):
        j = i // 2
        h = _ffn(h, ffn_pre_norm[i], *_prep_ffn_weights(ffn_pre_w_gu, ffn_pre_w_down, i))
        if i % 2 == 0:
            w_main, w_ba, alog_g, dtb_g = _prep_gdn_weights(dn_w_in, j, dn_a_log[j], dn_dt_bias[j])
            proj, ba = _norm_proj(h, mix_norm[i], w_main, w_ba)
            o = _gdn_mix(proj, ba, dn_conv_w[j], alog_g, dtb_g, dn_out_norm[j], batch, seq)
            h = _out_proj(o, _cast_bf16(dn_w_out, j, 0, d), h)
        else:
            if tables is None:
                tables = _rope_tables(seq)
            w_qkv = _cast_bf16(swa_w_qkv, j, 0, swa_w_qkv.shape[2])
            qkv = _swa_proj(h, mix_norm[i], w_qkv, tables, seq)
            o = _swa_attention(qkv, swa_sinks[j], batch, seq)
            h = _out_proj(o, _cast_bf16(swa_w_out, j, 0, d), h)
        h = _ffn(h, ffn_post_norm[i], *_prep_ffn_weights(ffn_post_w_gu, ffn_post_w_down, i))
    return _final_norm(h, final_norm, batch, seq)
```

```python
import functools
import math

import jax
import jax.numpy as jnp
from jax import lax
from jax.experimental import pallas as pl
from jax.experimental.pallas import tpu as pltpu

F32 = jnp.float32
BF16 = jnp.bfloat16

N_META = 16
NORM_EPS = 1e-6
L2_EPS = 1e-6
FFN_RES = 0.5
DN_QK_HEADS = 16
DN_V_HEADS = 32
DN_HEAD = 128
DN_CONV = 4
DN_CHUNK = 64
DN_KEY_DIM = DN_QK_HEADS * DN_HEAD
DN_VAL_DIM = DN_V_HEADS * DN_HEAD
SWA_Q_HEADS = 32
SWA_KV_HEADS = 4
SWA_HEAD_DIM = 64
SWA_GROUP = SWA_Q_HEADS // SWA_KV_HEADS
SWA_WINDOW = 128
ROPE_THETA = 10000.0
LOG2_E = math.log2(math.e)

LANES = 128
SWA_KEYS = 3 * LANES
assert SWA_KEYS > 2 * SWA_WINDOW + N_META
VMEM_LIMIT = 56 * 1024 * 1024
ROW_TILE_MAX = 688
FF_TILE = 512
PROJ_TILE = 2048
CAST_TILE_BYTES = 6 * 1024 * 1024
CAST_T_ROWS = 512
CONV_HALO = 8
DN_QK_PER_STEP = 16


def _cparams(*sem):
    return pltpu.CompilerParams(dimension_semantics=sem, vmem_limit_bytes=VMEM_LIMIT)


def _row_tile(rows):
    best = None
    for t in range(16, min(rows, ROW_TILE_MAX) + 1, 16):
        if rows % t == 0:
            best = t
    assert best is not None, rows
    return best


def _rms(x, gain):
    return x * lax.rsqrt(jnp.mean(x * x, axis=-1, keepdims=True) + NORM_EPS) * gain


def _silu(x):
    half = 0.5 * x
    return half + half * jnp.tanh(half)


def _mm(a, b):
    return jnp.dot(a.astype(BF16), b.astype(BF16), preferred_element_type=F32)


def _ffn_kernel(h_ref, gain_ref, wg_ref, wu_ref, wd_ref, o_ref, xn_ref, *, d_ff):
    j = pl.program_id(1)

    @pl.when(j == 0)
    def _():
        h = h_ref[...]
        xn_ref[...] = _rms(h, gain_ref[...]).astype(BF16)
        o_ref[...] = h

    xn = xn_ref[...]
    g = jnp.dot(xn, wg_ref[...], preferred_element_type=F32)
    u = jnp.dot(xn, wu_ref[...], preferred_element_type=F32)
    a = _silu(g) * u
    wd = wd_ref[...]
    if d_ff % FF_TILE:
        col = j * FF_TILE + lax.broadcasted_iota(jnp.int32, (1, FF_TILE), 1)
        a = jnp.where(col < d_ff, a, 0.0)
        row = j * FF_TILE + lax.broadcasted_iota(jnp.int32, (FF_TILE, 1), 0)
        wd = jnp.where(row < d_ff, wd, jnp.zeros_like(wd))
    o_ref[...] += FFN_RES * jnp.dot(a.astype(BF16), wd, preferred_element_type=F32)


def _ffn(h, gain, wg, wu, wd):
    rows, d = h.shape
    d_ff = wd.shape[0]
    tm = _row_tile(rows)
    return pl.pallas_call(
        functools.partial(_ffn_kernel, d_ff=d_ff),
        grid=(rows // tm, pl.cdiv(d_ff, FF_TILE)),
        in_specs=[
            pl.BlockSpec((tm, d), lambda i, j: (i, 0)),
            pl.BlockSpec((1, d), lambda i, j: (0, 0)),
            pl.BlockSpec((d, FF_TILE), lambda i, j: (0, j)),
            pl.BlockSpec((d, FF_TILE), lambda i, j: (0, j)),
            pl.BlockSpec((FF_TILE, d), lambda i, j: (j, 0)),
        ],
        out_specs=pl.BlockSpec((tm, d), lambda i, j: (i, 0)),
        out_shape=jax.ShapeDtypeStruct((rows, d), F32),
        scratch_shapes=[pltpu.VMEM((tm, d), BF16)],
        compiler_params=_cparams("parallel", "arbitrary"),
        name="ffn",
    )(h, gain.reshape(1, d), wg, wu, wd)


def _cast_kernel(w_ref, o_ref, *, valid_cols):
    w = w_ref[...]
    if valid_cols < w.shape[1]:
        col = lax.broadcasted_iota(jnp.int32, (1, w.shape[1]), 1)
        w = jnp.where(col < valid_cols, w, 0.0)
    o_ref[...] = w.astype(o_ref.dtype)


def _cast_bf16(w, layer, col_block, n_cols, valid_cols=None):
    rows = w.shape[1]
    valid_cols = n_cols if valid_cols is None else valid_cols
    tr = None
    for t in range(16, rows + 1, 16):
        if rows % t == 0 and t * n_cols * 4 <= CAST_TILE_BYTES:
            tr = t
    assert tr is not None, (rows, n_cols)
    return pl.pallas_call(
        functools.partial(_cast_kernel, valid_cols=valid_cols),
        grid=(rows // tr,),
        in_specs=[pl.BlockSpec((None, tr, n_cols), lambda r: (layer, r, col_block))],
        out_specs=pl.BlockSpec((tr, n_cols), lambda r: (r, 0)),
        out_shape=jax.ShapeDtypeStruct((rows, n_cols), BF16),
        compiler_params=_cparams("parallel"),
        name="cast_bf16",
    )(w)


def _cast_t_kernel(w_ref, o_ref, *, valid):
    w = w_ref[...].T
    if valid < w.shape[1]:
        col = lax.broadcasted_iota(jnp.int32, (1, w.shape[1]), 1)
        w = jnp.where(col < valid, w, 0.0)
    o_ref[...] = w.astype(o_ref.dtype)


def _cast_t_bf16(wt, layer, row_start, n_rows, tr):
    k = wt.shape[2]
    n_out = -(-n_rows // tr) * tr
    assert row_start % tr == 0
    first = row_start // tr
    valid = tr if n_rows % tr == 0 else n_rows % tr
    assert valid == tr or n_out == tr
    return pl.pallas_call(
        functools.partial(_cast_t_kernel, valid=valid),
        grid=(n_out // tr,),
        in_specs=[pl.BlockSpec((None, tr, k), lambda r: (layer, first + r, 0))],
        out_specs=pl.BlockSpec((k, tr), lambda r: (0, r)),
        out_shape=jax.ShapeDtypeStruct((k, n_out), BF16),
        compiler_params=_cparams("parallel"),
        name="cast_t_bf16",
    )(wt)


def _prep_ffn_weights(w_gu, w_down, layer):
    d_ff = w_down.shape[1]
    return (_cast_bf16(w_gu, layer, 0, d_ff), _cast_bf16(w_gu, layer, 1, d_ff),
            _cast_bf16(w_down, layer, 0, w_down.shape[2]))


def _proj_kernel(h_ref, gain_ref, w_ref, ws_ref, o_ref, os_ref, xn_ref):
    @pl.when(pl.program_id(1) == 0)
    def _():
        xn = _rms(h_ref[...], gain_ref[...]).astype(BF16)
        xn_ref[...] = xn
        os_ref[...] = jnp.dot(xn, ws_ref[...], preferred_element_type=F32)

    o_ref[...] = jnp.dot(xn_ref[...], w_ref[...], preferred_element_type=F32)


def _norm_proj(h, gain, w, w_small):
    rows, d = h.shape
    n, ns = w.shape[1], w_small.shape[1]
    tm = _row_tile(rows)
    tn = PROJ_TILE
    assert n % tn == 0
    return pl.pallas_call(
        _proj_kernel,
        grid=(rows // tm, n // tn),
        in_specs=[
            pl.BlockSpec((tm, d), lambda i, j: (i, 0)),
            pl.BlockSpec((1, d), lambda i, j: (0, 0)),
            pl.BlockSpec((d, tn), lambda i, j: (0, j)),
            pl.BlockSpec((d, ns), lambda i, j: (0, 0)),
        ],
        out_specs=[
            pl.BlockSpec((tm, tn), lambda i, j: (i, j)),
            pl.BlockSpec((tm, ns), lambda i, j: (i, 0)),
        ],
        out_shape=[
            jax.ShapeDtypeStruct((rows, n), F32),
            jax.ShapeDtypeStruct((rows, ns), F32),
        ],
        scratch_shapes=[pltpu.VMEM((tm, d), BF16)],
        compiler_params=_cparams("parallel", "arbitrary"),
        name="norm_proj",
    )(h, gain.reshape(1, d), w, w_small)


def _rope_table_kernel(c_ref, s1_ref, s2_ref):
    shape = c_ref.shape
    pos = lax.broadcasted_iota(jnp.int32, shape, 0).astype(F32)
    dim = lax.broadcasted_iota(jnp.int32, shape, 1) % SWA_HEAD_DIM
    half = SWA_HEAD_DIM // 2
    k = (dim % half).astype(F32)
    inv = jnp.exp(k * (-2.0 / SWA_HEAD_DIM * math.log(ROPE_THETA)))
    ang = pos * inv
    s = jnp.sin(ang)
    c_ref[...] = jnp.cos(ang)
    s1_ref[...] = jnp.where(dim < half, -s, 0.0)
    s2_ref[...] = jnp.where(dim >= half, s, 0.0)


def _rope_tables(seq):
    shp = jax.ShapeDtypeStruct((seq, LANES), F32)
    return pl.pallas_call(_rope_table_kernel, out_shape=(shp, shp, shp), name="rope_tables")()


def _swa_proj_kernel(h_ref, gain_ref, w_ref, c_ref, s1_ref, s2_ref, o_ref, *, n_rope):
    xn = _rms(h_ref[...], gain_ref[...]).astype(BF16)
    y = jnp.dot(xn, w_ref[...], preferred_element_type=F32)
    c, s1, s2 = c_ref[...], s1_ref[...], s2_ref[...]
    half = SWA_HEAD_DIM // 2
    for ch in range(y.shape[1] // LANES):
        yc = y[:, ch * LANES:(ch + 1) * LANES]
        if ch < n_rope:
            yc = (yc * c + pltpu.roll(yc, LANES - half, axis=1) * s1
                  + pltpu.roll(yc, half, axis=1) * s2)
        o_ref[:, ch * LANES:(ch + 1) * LANES] = yc.astype(o_ref.dtype)


def _swa_proj(h, gain, w, tables, seq):
    rows, d = h.shape
    n = w.shape[1]
    tm = _row_tile(seq)
    per_seq = seq // tm
    n_rope = (SWA_Q_HEADS + SWA_KV_HEADS) * SWA_HEAD_DIM // LANES
    tab = pl.BlockSpec((tm, LANES), lambda i: (i % per_seq, 0))
    return pl.pallas_call(
        functools.partial(_swa_proj_kernel, n_rope=n_rope),
        grid=(rows // tm,),
        in_specs=[
            pl.BlockSpec((tm, d), lambda i: (i, 0)),
            pl.BlockSpec((1, d), lambda i: (0, 0)),
            pl.BlockSpec((d, n), lambda i: (0, 0)),
            tab, tab, tab,
        ],
        out_specs=pl.BlockSpec((tm, n), lambda i: (i, 0)),
        out_shape=jax.ShapeDtypeStruct((rows, n), BF16),
        compiler_params=_cparams("parallel"),
        name="swa_proj",
    )(h, gain.reshape(1, d), w, *tables)


def _out_proj_kernel(a_ref, w_ref, h_ref, o_ref):
    o_ref[...] = h_ref[...] + jnp.dot(a_ref[...], w_ref[...], preferred_element_type=F32)


def _out_proj(a, w, h, tn=1024):
    rows, k = a.shape
    d = w.shape[1]
    tm = _row_tile(rows)
    return pl.pallas_call(
        _out_proj_kernel,
        grid=(rows // tm, d // tn),
        in_specs=[
            pl.BlockSpec((tm, k), lambda i, j: (i, 0)),
            pl.BlockSpec((k, tn), lambda i, j: (0, j)),
            pl.BlockSpec((tm, tn), lambda i, j: (i, j)),
        ],
        out_specs=pl.BlockSpec((tm, tn), lambda i, j: (i, j)),
        out_shape=jax.ShapeDtypeStruct((rows, d), F32),
        compiler_params=_cparams("parallel", "arbitrary"),
        name="out_proj",
    )(a, w, h)


def _swa_kernel(sink_ref, q_ref, kp_ref, kc_ref, km_ref, vp_ref, vc_ref, vm_ref, o_ref, *, seq):
    m = pl.program_id(1)
    w, hd, nm = SWA_WINDOW, SWA_HEAD_DIM, N_META
    nk = SWA_KEYS
    n_real = 2 * w + nm
    ii = lax.broadcasted_iota(jnp.int32, (w, nk), 0)
    jj = lax.broadcasted_iota(jnp.int32, (w, nk), 1)
    band = (jj > ii) & (jj <= ii + w) & ((m > 0) | (jj >= w))
    meta = (jj >= 2 * w) & (jj < n_real) & (m * w + ii - (jj - 2 * w) >= w)
    mask = band | meta
    is_sink = lax.broadcasted_iota(jnp.int32, (1, nk), 1) == n_real
    krow = lax.broadcasted_iota(jnp.int32, (n_real, 1), 0)
    kabs = jnp.where(krow < 2 * w, (m - 1) * w + krow, krow - 2 * w)
    kvalid = (kabs >= 0) & (kabs < seq)
    k_all = jnp.concatenate([kp_ref[0], kc_ref[0], km_ref[0, :nm, :]], axis=0)
    v_all = jnp.concatenate([vp_ref[0], vc_ref[0], vm_ref[0, :nm, :]], axis=0)
    fill = jnp.zeros((nk - n_real, k_all.shape[1]), k_all.dtype)
    k_all = jnp.concatenate([jnp.where(kvalid, k_all, jnp.zeros_like(k_all)), fill], axis=0)
    v_all = jnp.concatenate([jnp.where(kvalid, v_all, jnp.zeros_like(v_all)), fill], axis=0)
    ones = jnp.ones((nk, hd), k_all.dtype)
    zeros = jnp.zeros((nk, hd), k_all.dtype)
    q = q_ref[0]
    npair = SWA_GROUP // 2

    def block_diag(x, extra_lo, extra_hi):
        lo = jnp.concatenate([x, jnp.zeros_like(x)] + extra_lo, axis=1)
        hi = jnp.concatenate([jnp.zeros_like(x), x] + extra_hi, axis=1)
        return jnp.concatenate([lo, hi], axis=0)

    scores = []
    for kv in range(SWA_KV_HEADS):
        k_bd = block_diag(k_all[:, kv * hd:(kv + 1) * hd], [], [])
        q3 = jnp.stack([q[:, (kv * npair + j) * 2 * hd:(kv * npair + j + 1) * 2 * hd]
                        for j in range(npair)], axis=0)
        scores.append(jnp.einsum("pqd,pkd->pqk", q3, jnp.stack([k_bd] * npair, axis=0),
                                 preferred_element_type=F32))
    for kv in range(SWA_KV_HEADS):
        v_bd = block_diag(v_all[:, kv * hd:(kv + 1) * hd], [ones, zeros], [zeros, ones])
        probs = []
        for half in range(2):
            s = scores[kv][:, :, half * nk:(half + 1) * nk]
            sink = jnp.stack([sink_ref[kv * SWA_GROUP + 2 * j + half] for j in range(npair)], axis=0)
            s = jnp.where(mask, s * (hd ** -0.5 * LOG2_E), jnp.where(is_sink, sink * LOG2_E, -jnp.inf))
            probs.append(jnp.exp2(s - jnp.max(s, axis=-1, keepdims=True)).astype(BF16))
        p = jnp.concatenate(probs, axis=-1)
        o = jnp.einsum("pqk,pkd->pqd", p, jnp.stack([v_bd] * npair, axis=0),
                       preferred_element_type=F32)
        o = o[:, :, :2 * hd] / o[:, :, 2 * hd:]
        for j in range(npair):
            c0 = (kv * npair + j) * 2 * hd
            o_ref[0, :, c0:c0 + 2 * hd] = o[j].astype(o_ref.dtype)


def _swa_attention(qkv, sinks, batch, seq):
    n = qkv.shape[1]
    qkv = qkv.reshape(batch, seq, n)
    w = SWA_WINDOW
    nq = SWA_Q_HEADS * SWA_HEAD_DIM
    nkv = SWA_KV_HEADS * SWA_HEAD_DIM
    kcol = nq // nkv
    vcol = kcol + 1
    nblk = pl.cdiv(seq, w)

    def kv_spec(col, which):
        if which == "prev":
            return pl.BlockSpec((1, w, nkv), lambda b, m: (b, jnp.maximum(m - 1, 0), col))
        if which == "cur":
            return pl.BlockSpec((1, w, nkv), lambda b, m: (b, m, col))
        return pl.BlockSpec((1, w, nkv), lambda b, m: (b, 0, col))

    out = pl.pallas_call(
        functools.partial(_swa_kernel, seq=seq),
        grid=(batch, nblk),
        in_specs=[
            pl.BlockSpec((SWA_Q_HEADS, 1, 1), lambda b, m: (0, 0, 0)),
            pl.BlockSpec((1, w, nq), lambda b, m: (b, m, 0)),
            kv_spec(kcol, "prev"), kv_spec(kcol, "cur"), kv_spec(kcol, "meta"),
            kv_spec(vcol, "prev"), kv_spec(vcol, "cur"), kv_spec(vcol, "meta"),
        ],
        out_specs=pl.BlockSpec((1, w, nq), lambda b, m: (b, m, 0)),
        out_shape=jax.ShapeDtypeStruct((batch, seq, nq), BF16),
        compiler_params=_cparams("parallel", "parallel"),
        name="swa_attention",
    )(sinks.astype(F32).reshape(SWA_Q_HEADS, 1, 1), qkv, qkv, qkv, qkv, qkv, qkv, qkv)
    return out.reshape(batch * seq, nq)


def _bmm(a, b):
    return jnp.einsum("hmk,hkn->hmn", a.astype(BF16), b.astype(BF16), preferred_element_type=F32)


def _bmm_nt(a, b):
    return jnp.einsum("hmk,hnk->hmn", a.astype(BF16), b.astype(BF16), preferred_element_type=F32)


def _bmm_tn(a, b):
    return jnp.einsum("hkm,hkn->hmn", a.astype(BF16), b.astype(BF16), preferred_element_type=F32)


def _tri_inverse(a, ii, jj):
    c = a.shape[-1]
    eye = jnp.where(ii == jj, 1.0, 0.0)
    inv = eye - jnp.where((ii >> 1) == (jj >> 1), a, 0.0)
    lg = 1
    while (1 << lg) < c:
        same_pair = (ii >> (lg + 1)) == (jj >> (lg + 1))
        off = jnp.where(same_pair & ((ii >> lg) != (jj >> lg)), a, 0.0)
        inv = inv - _bmm(inv, _bmm(off, inv))
        lg += 1
    return inv


def _conv_silu(x_ref, tail_ref, w_ref):
    x = x_ref[0]
    c = x.shape[0]
    nt = tail_ref.shape[0]
    xe = jnp.concatenate([tail_ref[...], x], axis=0)
    w = w_ref[...]
    y = x * w[DN_CONV - 1:DN_CONV, :]
    for s in range(1, DN_CONV):
        y = y + pltpu.roll(xe, s, axis=0)[nt:, :] * w[DN_CONV - 1 - s:DN_CONV - s, :]
    tail_ref[...] = x[c - nt:, :]
    return _silu(y)


def _gdn_kernel(q_ref, k_ref, v_ref, z_ref, ba_ref, cwq_ref, cwk_ref, cwv_ref, alog_ref, dtb_ref,
                ogain_ref, o_ref, s_ref, tq_ref, tk_ref, tv_ref, *, nq, seq):
    c_idx = pl.program_id(2)
    nv = 2 * nq
    c, hd = DN_CHUNK, DN_HEAD

    @pl.when(c_idx == 0)
    def _():
        s_ref[...] = jnp.zeros_like(s_ref)
        tq_ref[...] = jnp.zeros_like(tq_ref)
        tk_ref[...] = jnp.zeros_like(tk_ref)
        tv_ref[...] = jnp.zeros_like(tv_ref)

    row = c_idx * c + lax.broadcasted_iota(jnp.int32, (c, 1), 0)
    valid = row < seq
    ii = lax.broadcasted_iota(jnp.int32, (c, c), 0)
    jj = lax.broadcasted_iota(jnp.int32, (c, c), 1)
    lower = ii >= jj

    qc = jnp.where(valid, _conv_silu(q_ref, tq_ref, cwq_ref), 0.0)
    kc = jnp.where(valid, _conv_silu(k_ref, tk_ref, cwk_ref), 0.0)
    vc = jnp.where(valid, _conv_silu(v_ref, tv_ref, cwv_ref), 0.0)

    ba = ba_ref[0]
    beta = jnp.where(valid, jax.nn.sigmoid(ba), 0.0)
    x = ba + dtb_ref[0]
    softplus = jnp.maximum(x, 0.0) + jnp.log1p(jnp.exp(-jnp.abs(x)))
    g = jnp.where(valid, -jnp.exp(alog_ref[0]) * softplus, 0.0)
    rr = lax.broadcasted_iota(jnp.int32, g.shape, 0)
    gc = g
    s = 1
    while s < c:
        gc = gc + jnp.where(rr >= s, pltpu.roll(gc, s, axis=0), 0.0)
        s *= 2
    gct = gc.T
    glast = gc[c - 1:c, :]

    def heads(x, n):
        return jnp.stack([x[:, h * hd:(h + 1) * hd] for h in range(n)], axis=0)

    def per_v_head(x):
        return jnp.stack([x[l // 2] for l in range(nv)], axis=0)

    qh = heads(qc, nq)
    kh = heads(kc, nq)
    qh = qh * lax.rsqrt(jnp.sum(qh * qh, axis=-1, keepdims=True) + L2_EPS) * (hd ** -0.5)
    kh = kh * lax.rsqrt(jnp.sum(kh * kh, axis=-1, keepdims=True) + L2_EPS)
    kk = per_v_head(_bmm_nt(kh, kh))
    qk = per_v_head(_bmm_nt(qh, kh))
    qv = per_v_head(qh)
    kv = per_v_head(kh)
    vh = heads(vc, nv)
    zh = heads(z_ref[0], nv)
    bi = jnp.stack([beta[:, l:l + 1] for l in range(nv)], axis=0)
    gi = jnp.stack([gc[:, nv + l:nv + l + 1] for l in range(nv)], axis=0)
    gj = jnp.stack([gct[nv + l:nv + l + 1, :] for l in range(nv)], axis=0)
    gl = jnp.stack([glast[:, nv + l:nv + l + 1] for l in range(nv)], axis=0)

    decay = jnp.where(lower, jnp.exp(jnp.where(lower, gi - gj, 0.0)), 0.0)
    a = jnp.where(ii > jj, bi * kk * decay, 0.0)
    t = _tri_inverse(a, ii, jj)
    egi = jnp.exp(gi)
    kb = kv * bi
    sol = _bmm(t, jnp.concatenate([vh * bi, kb * egi], axis=-1))
    u, wmat = sol[:, :, :hd], sol[:, :, hd:]
    state = s_ref[...]
    v_new = u - _bmm(wmat, state)
    out = _bmm(qv * egi, state) + _bmm(qk * decay, v_new)
    s_ref[...] = state * jnp.exp(gl) + _bmm_tn(kv * jnp.exp(gl - gi), v_new)
    o = _rms(out, ogain_ref[...]) * _silu(zh)
    for l in range(nv):
        o_ref[0, :, l * hd:(l + 1) * hd] = o[l].astype(o_ref.dtype)


def _gdn_mix(proj, ba, conv_w, alog_g, dtb_g, o_gain, batch, seq):
    nq = DN_QK_PER_STEP
    nv = 2 * nq
    groups = DN_QK_HEADS // nq
    proj = proj.reshape(batch, seq, proj.shape[1])
    ba = ba.reshape(batch, seq, ba.shape[1])
    c = DN_CHUNK
    wq, wv = nq * DN_HEAD, nv * DN_HEAD
    k_off = DN_KEY_DIM // wq
    v_off = 2 * DN_KEY_DIM // wv
    z_off = (2 * DN_KEY_DIM + DN_VAL_DIM) // wv

    def col(width, off):
        return pl.BlockSpec((1, c, width), lambda b, g, t: (b, t, off + g))

    def cw(width, off):
        return pl.BlockSpec((DN_CONV, width), lambda b, g, t: (0, off + g))

    gate = pl.BlockSpec((1, 1, LANES), lambda b, g, t: (g, 0, 0))
    out = pl.pallas_call(
        functools.partial(_gdn_kernel, nq=nq, seq=seq),
        grid=(batch, groups, pl.cdiv(seq, c)),
        in_specs=[
            col(wq, 0), col(wq, k_off), col(wv, v_off), col(wv, z_off), col(LANES, 0),
            cw(wq, 0), cw(wq, k_off), cw(wv, v_off),
            gate, gate,
            pl.BlockSpec((1, DN_HEAD), lambda b, g, t: (0, 0)),
        ],
        out_specs=pl.BlockSpec((1, c, wv), lambda b, g, t: (b, t, g)),
        out_shape=jax.ShapeDtypeStruct((batch, seq, DN_VAL_DIM), BF16),
        scratch_shapes=[
            pltpu.VMEM((nv, DN_HEAD, DN_HEAD), F32),
            pltpu.VMEM((CONV_HALO, wq), F32),
            pltpu.VMEM((CONV_HALO, wq), F32),
            pltpu.VMEM((CONV_HALO, wv), F32),
        ],
        compiler_params=_cparams("parallel", "parallel", "arbitrary"),
        name="gdn_mix",
    )(proj, proj, proj, proj, ba, conv_w, conv_w, conv_w, alog_g, dtb_g,
      o_gain.reshape(1, DN_HEAD))
    return out.reshape(batch * seq, DN_VAL_DIM)


def _prep_gdn_weights(w_in_all, layer, a_log, dt_bias):
    nq = DN_QK_PER_STEP
    nv = 2 * nq
    groups = DN_QK_HEADS // nq
    main = 2 * DN_KEY_DIM + 2 * DN_VAL_DIM
    assert groups == 1 and main % LANES == 0 and 2 * nv <= LANES
    wt = jnp.swapaxes(w_in_all, 1, 2)
    w_ba = _cast_t_bf16(wt, layer, main, 2 * nv, LANES)
    w_main = _cast_t_bf16(wt, layer, 0, main, CAST_T_ROWS)

    def gate_rows(p):
        p = p.astype(F32).reshape(groups, nv)
        z = jnp.zeros((groups, nv), F32)
        zp = jnp.zeros((groups, LANES - 2 * nv), F32)
        return jnp.concatenate([z, p, zp], axis=1).reshape(groups, 1, LANES)

    return w_main, w_ba, gate_rows(a_log), gate_rows(dt_bias)


def _final_norm_kernel(h_ref, gain_ref, o_ref):
    o_ref[...] = _rms(h_ref[0], gain_ref[...])


def _final_norm(h, gain, batch, seq):
    d = h.shape[1]
    h = h.reshape(batch, seq, d)
    n_out = seq - N_META
    tr = 256 if n_out % 256 == 0 else n_out
    return pl.pallas_call(
        _final_norm_kernel,
        grid=(batch, n_out // tr),
        in_specs=[
            pl.BlockSpec((pl.Element(1), pl.Element(tr), pl.Element(d)),
                         lambda b, r: (b, pl.multiple_of(N_META + r * tr, 8), 0)),
            pl.BlockSpec((1, d), lambda b, r: (0, 0)),
        ],
        out_specs=pl.BlockSpec((None, tr, d), lambda b, r: (b, r, 0)),
        out_shape=jax.ShapeDtypeStruct((batch, n_out, d), F32),
        compiler_params=_cparams("parallel", "parallel"),
        name="final_norm",
    )(h, gain.reshape(1, d))


def kernel(x, meta_tokens, ffn_pre_norm, ffn_pre_w_gu, ffn_pre_w_down, mix_norm, ffn_post_norm,
           ffn_post_w_gu, ffn_post_w_down, dn_w_in, dn_conv_w, dn_a_log, dn_dt_bias, dn_out_norm,
           dn_w_out, swa_w_qkv, swa_sinks, swa_w_out, final_norm):
    batch, seq_in, d = x.shape
    seq = seq_in + N_META
    depth = ffn_pre_norm.shape[0]
    meta = jnp.broadcast_to(meta_tokens[None].astype(x.dtype), (batch, N_META, d))
    h = jnp.concatenate([meta, x], axis=1).reshape(batch * seq, d)
    tables = None
    for i in range(depth):
        j = i // 2
        h = _ffn(h, ffn_pre_norm[i], *_prep_ffn_weights(ffn_pre_w_gu, ffn_pre_w_down, i))
        if i % 2 == 0:
            w_main, w_ba, alog_g, dtb_g = _prep_gdn_weights(dn_w_in, j, dn_a_log[j], dn_dt_bias[j])
            proj, ba = _norm_proj(h, mix_norm[i], w_main, w_ba)
            o = _gdn_mix(proj, ba, dn_conv_w[j], alog_g, dtb_g, dn_out_norm[j], batch, seq)
            h = _out_proj(o, _cast_bf16(dn_w_out, j, 0, d), h)
        else:
            if tables is None:
                tables = _rope_tables(seq)
            w_qkv = _cast_bf16(swa_w_qkv, j, 0, swa_w_qkv.shape[2])
            qkv = _swa_proj(h, mix_norm[i], w_qkv, tables, seq)
            o = _swa_attention(qkv, swa_sinks[j], batch, seq)
            h = _out_proj(o, _cast_bf16(swa_w_out, j, 0, d), h)
        h = _ffn(h, ffn_post_norm[i], *_prep_ffn_weights(ffn_post_w_gu, ffn_post_w_down, i))
    return _final_norm(h, final_norm, batch, seq)
```

```python
import functools
import math

import jax
import jax.numpy as jnp
from jax import lax
from jax.experimental import pallas as pl
from jax.experimental.pallas import tpu as pltpu

F32 = jnp.float32
BF16 = jnp.bfloat16

N_META = 16
NORM_EPS = 1e-6
L2_EPS = 1e-6
FFN_RES = 0.5
DN_QK_HEADS = 16
DN_V_HEADS = 32
DN_HEAD = 128
DN_CONV = 4
DN_CHUNK = 64
DN_KEY_DIM = DN_QK_HEADS * DN_HEAD
DN_VAL_DIM = DN_V_HEADS * DN_HEAD
SWA_Q_HEADS = 32
SWA_KV_HEADS = 4
SWA_HEAD_DIM = 64
SWA_GROUP = SWA_Q_HEADS // SWA_KV_HEADS
SWA_WINDOW = 128
ROPE_THETA = 10000.0
LOG2_E = math.log2(math.e)

LANES = 128
SWA_KEYS = 3 * LANES
assert SWA_KEYS > 2 * SWA_WINDOW + N_META
VMEM_LIMIT = 56 * 1024 * 1024
ROW_TILE_MAX = 688
FF_TILE = 512
PROJ_TILE = 2048
CAST_TILE_BYTES = 6 * 1024 * 1024
CAST_T_ROWS = 512
CONV_HALO = 8
DN_QK_PER_STEP = 16


def _cparams(*sem):
    return pltpu.CompilerParams(dimension_semantics=sem, vmem_limit_bytes=VMEM_LIMIT)


def _row_tile(rows):
    best = None
    for t in range(16, min(rows, ROW_TILE_MAX) + 1, 16):
        if rows % t == 0:
            best = t
    assert best is not None, rows
    return best


def _rms(x, gain):
    return x * lax.rsqrt(jnp.mean(x * x, axis=-1, keepdims=True) + NORM_EPS) * gain


def _silu(x):
    half = 0.5 * x
    return half + half * jnp.tanh(half)


def _mm(a, b):
    return jnp.dot(a.astype(BF16), b.astype(BF16), preferred_element_type=F32)


def _ffn_kernel(h_ref, gain_ref, wg_ref, wu_ref, wd_ref, o_ref, xn_ref, *, d_ff):
    j = pl.program_id(1)

    @pl.when(j == 0)
    def _():
        h = h_ref[...]
        xn_ref[...] = _rms(h, gain_ref[...]).astype(BF16)
        o_ref[...] = h

    xn = xn_ref[...]
    g = jnp.dot(xn, wg_ref[...], preferred_element_type=F32)
    u = jnp.dot(xn, wu_ref[...], preferred_element_type=F32)
    a = _silu(g) * (FFN_RES * u)
    wd = wd_ref[...]
    if d_ff % FF_TILE:
        col = j * FF_TILE + lax.broadcasted_iota(jnp.int32, (1, FF_TILE), 1)
        a = jnp.where(col < d_ff, a, 0.0)
        row = j * FF_TILE + lax.broadcasted_iota(jnp.int32, (FF_TILE, 1), 0)
        wd = jnp.where(row < d_ff, wd, 0.0)
    o_ref[...] += jnp.dot(a.astype(BF16), wd.astype(BF16), preferred_element_type=F32)


def _ffn(h, gain, wg, wu, wd_all, layer):
    rows, d = h.shape
    d_ff = wd_all.shape[1]
    tm = _row_tile(rows)
    return pl.pallas_call(
        functools.partial(_ffn_kernel, d_ff=d_ff),
        grid=(rows // tm, pl.cdiv(d_ff, FF_TILE)),
        in_specs=[
            pl.BlockSpec((tm, d), lambda i, j: (i, 0)),
            pl.BlockSpec((1, d), lambda i, j: (0, 0)),
            pl.BlockSpec((d, FF_TILE), lambda i, j: (0, j)),
            pl.BlockSpec((d, FF_TILE), lambda i, j: (0, j)),
            pl.BlockSpec((None, FF_TILE, d), lambda i, j: (layer, j, 0)),
        ],
        out_specs=pl.BlockSpec((tm, d), lambda i, j: (i, 0)),
        out_shape=jax.ShapeDtypeStruct((rows, d), F32),
        scratch_shapes=[pltpu.VMEM((tm, d), BF16)],
        compiler_params=_cparams("parallel", "arbitrary"),
        name="ffn",
    )(h, gain.reshape(1, d), wg, wu, wd_all)


def _cast_kernel(w_ref, o_ref, *, valid_cols):
    w = w_ref[...]
    if valid_cols < w.shape[1]:
        col = lax.broadcasted_iota(jnp.int32, (1, w.shape[1]), 1)
        w = jnp.where(col < valid_cols, w, 0.0)
    o_ref[...] = w.astype(o_ref.dtype)


def _cast_bf16(w, layer, col_block, n_cols, valid_cols=None):
    rows = w.shape[1]
    valid_cols = n_cols if valid_cols is None else valid_cols
    tr = None
    for t in range(16, rows + 1, 16):
        if rows % t == 0 and t * n_cols * 4 <= CAST_TILE_BYTES:
            tr = t
    assert tr is not None, (rows, n_cols)
    return pl.pallas_call(
        functools.partial(_cast_kernel, valid_cols=valid_cols),
        grid=(rows // tr,),
        in_specs=[pl.BlockSpec((None, tr, n_cols), lambda r: (layer, r, col_block))],
        out_specs=pl.BlockSpec((tr, n_cols), lambda r: (r, 0)),
        out_shape=jax.ShapeDtypeStruct((rows, n_cols), BF16),
        compiler_params=_cparams("parallel"),
        name="cast_bf16",
    )(w)


def _cast_t_kernel(w_ref, o_ref, *, valid):
    w = w_ref[...].T
    if valid < w.shape[1]:
        col = lax.broadcasted_iota(jnp.int32, (1, w.shape[1]), 1)
        w = jnp.where(col < valid, w, 0.0)
    o_ref[...] = w.astype(o_ref.dtype)


def _cast_t_bf16(wt, layer, row_start, n_rows, tr):
    k = wt.shape[2]
    n_out = -(-n_rows // tr) * tr
    assert row_start % tr == 0
    first = row_start // tr
    valid = tr if n_rows % tr == 0 else n_rows % tr
    assert valid == tr or n_out == tr
    return pl.pallas_call(
        functools.partial(_cast_t_kernel, valid=valid),
        grid=(n_out // tr,),
        in_specs=[pl.BlockSpec((None, tr, k), lambda r: (layer, first + r, 0))],
        out_specs=pl.BlockSpec((k, tr), lambda r: (0, r)),
        out_shape=jax.ShapeDtypeStruct((k, n_out), BF16),
        compiler_params=_cparams("parallel"),
        name="cast_t_bf16",
    )(wt)


def _prep_ffn_weights(w_gu, w_down, layer):
    d_ff = w_down.shape[1]
    return _cast_bf16(w_gu, layer, 0, d_ff), _cast_bf16(w_gu, layer, 1, d_ff), w_down, layer


def _proj_kernel(h_ref, gain_ref, w_ref, ws_ref, o_ref, os_ref, xn_ref):
    @pl.when(pl.program_id(1) == 0)
    def _():
        xn = _rms(h_ref[...], gain_ref[...]).astype(BF16)
        xn_ref[...] = xn
        os_ref[...] = jnp.dot(xn, ws_ref[...], preferred_element_type=F32)

    o_ref[...] = jnp.dot(xn_ref[...], w_ref[...], preferred_element_type=F32)


def _norm_proj(h, gain, w, w_small):
    rows, d = h.shape
    n, ns = w.shape[1], w_small.shape[1]
    tm = _row_tile(rows)
    tn = PROJ_TILE
    assert n % tn == 0
    return pl.pallas_call(
        _proj_kernel,
        grid=(rows // tm, n // tn),
        in_specs=[
            pl.BlockSpec((tm, d), lambda i, j: (i, 0)),
            pl.BlockSpec((1, d), lambda i, j: (0, 0)),
            pl.BlockSpec((d, tn), lambda i, j: (0, j)),
            pl.BlockSpec((d, ns), lambda i, j: (0, 0)),
        ],
        out_specs=[
            pl.BlockSpec((tm, tn), lambda i, j: (i, j)),
            pl.BlockSpec((tm, ns), lambda i, j: (i, 0)),
        ],
        out_shape=[
            jax.ShapeDtypeStruct((rows, n), F32),
            jax.ShapeDtypeStruct((rows, ns), F32),
        ],
        scratch_shapes=[pltpu.VMEM((tm, d), BF16)],
        compiler_params=_cparams("parallel", "arbitrary"),
        name="norm_proj",
    )(h, gain.reshape(1, d), w, w_small)


def _rope_table_kernel(c_ref, s1_ref, s2_ref):
    shape = c_ref.shape
    pos = lax.broadcasted_iota(jnp.int32, shape, 0).astype(F32)
    dim = lax.broadcasted_iota(jnp.int32, shape, 1) % SWA_HEAD_DIM
    half = SWA_HEAD_DIM // 2
    k = (dim % half).astype(F32)
    inv = jnp.exp(k * (-2.0 / SWA_HEAD_DIM * math.log(ROPE_THETA)))
    ang = pos * inv
    s = jnp.sin(ang)
    c_ref[...] = jnp.cos(ang)
    s1_ref[...] = jnp.where(dim < half, -s, 0.0)
    s2_ref[...] = jnp.where(dim >= half, s, 0.0)


def _rope_tables(seq):
    shp = jax.ShapeDtypeStruct((seq, LANES), F32)
    return pl.pallas_call(_rope_table_kernel, out_shape=(shp, shp, shp), name="rope_tables")()


def _swa_proj_kernel(h_ref, gain_ref, w_ref, c_ref, s1_ref, s2_ref, o_ref, *, n_rope):
    xn = _rms(h_ref[...], gain_ref[...]).astype(BF16)
    y = jnp.dot(xn, w_ref[...], preferred_element_type=F32)
    c, s1, s2 = c_ref[...], s1_ref[...], s2_ref[...]
    half = SWA_HEAD_DIM // 2
    for ch in range(y.shape[1] // LANES):
        yc = y[:, ch * LANES:(ch + 1) * LANES]
        if ch < n_rope:
            yc = (yc * c + pltpu.roll(yc, LANES - half, axis=1) * s1
                  + pltpu.roll(yc, half, axis=1) * s2)
        o_ref[:, ch * LANES:(ch + 1) * LANES] = yc.astype(o_ref.dtype)


def _swa_proj(h, gain, w, tables, seq):
    rows, d = h.shape
    n = w.shape[1]
    tm = _row_tile(seq)
    per_seq = seq // tm
    n_rope = (SWA_Q_HEADS + SWA_KV_HEADS) * SWA_HEAD_DIM // LANES
    tab = pl.BlockSpec((tm, LANES), lambda i: (i % per_seq, 0))
    return pl.pallas_call(
        functools.partial(_swa_proj_kernel, n_rope=n_rope),
        grid=(rows // tm,),
        in_specs=[
            pl.BlockSpec((tm, d), lambda i: (i, 0)),
            pl.BlockSpec((1, d), lambda i: (0, 0)),
            pl.BlockSpec((d, n), lambda i: (0, 0)),
            tab, tab, tab,
        ],
        out_specs=pl.BlockSpec((tm, n), lambda i: (i, 0)),
        out_shape=jax.ShapeDtypeStruct((rows, n), BF16),
        compiler_params=_cparams("parallel"),
        name="swa_proj",
    )(h, gain.reshape(1, d), w, *tables)


def _out_proj_kernel(a_ref, w_ref, h_ref, o_ref):
    o_ref[...] = h_ref[...] + jnp.dot(a_ref[...], w_ref[...], preferred_element_type=F32)


def _out_proj(a, w, h, tn=1024):
    rows, k = a.shape
    d = w.shape[1]
    tm = _row_tile(rows)
    return pl.pallas_call(
        _out_proj_kernel,
        grid=(rows // tm, d // tn),
        in_specs=[
            pl.BlockSpec((tm, k), lambda i, j: (i, 0)),
            pl.BlockSpec((k, tn), lambda i, j: (0, j)),
            pl.BlockSpec((tm, tn), lambda i, j: (i, j)),
        ],
        out_specs=pl.BlockSpec((tm, tn), lambda i, j: (i, j)),
        out_shape=jax.ShapeDtypeStruct((rows, d), F32),
        compiler_params=_cparams("parallel", "arbitrary"),
        name="out_proj",
    )(a, w, h)


def _swa_kernel(sink_ref, q_ref, kp_ref, kc_ref, km_ref, vp_ref, vc_ref, vm_ref, o_ref, *, seq):
    m = pl.program_id(1)
    w, hd, nm = SWA_WINDOW, SWA_HEAD_DIM, N_META
    nk = SWA_KEYS
    n_real = 2 * w + nm
    ii = lax.broadcasted_iota(jnp.int32, (w, nk), 0)
    jj = lax.broadcasted_iota(jnp.int32, (w, nk), 1)
    band = (jj > ii) & (jj <= ii + w) & ((m > 0) | (jj >= w))
    meta = (jj >= 2 * w) & (jj < n_real) & (m * w + ii - (jj - 2 * w) >= w)
    mask = band | meta
    is_sink = lax.broadcasted_iota(jnp.int32, (1, nk), 1) == n_real
    krow = lax.broadcasted_iota(jnp.int32, (n_real, 1), 0)
    kabs = jnp.where(krow < 2 * w, (m - 1) * w + krow, krow - 2 * w)
    kvalid = (kabs >= 0) & (kabs < seq)
    k_all = jnp.concatenate([kp_ref[0], kc_ref[0], km_ref[0, :nm, :]], axis=0)
    v_all = jnp.concatenate([vp_ref[0], vc_ref[0], vm_ref[0, :nm, :]], axis=0)
    fill = jnp.zeros((nk - n_real, k_all.shape[1]), k_all.dtype)
    k_all = jnp.concatenate([jnp.where(kvalid, k_all, jnp.zeros_like(k_all)), fill], axis=0)
    v_all = jnp.concatenate([jnp.where(kvalid, v_all, jnp.zeros_like(v_all)), fill], axis=0)
    ones = jnp.ones((nk, hd), k_all.dtype)
    zeros = jnp.zeros((nk, hd), k_all.dtype)
    q = q_ref[0]
    npair = SWA_GROUP // 2

    def block_diag(x, extra_lo, extra_hi):
        lo = jnp.concatenate([x, jnp.zeros_like(x)] + extra_lo, axis=1)
        hi = jnp.concatenate([jnp.zeros_like(x), x] + extra_hi, axis=1)
        return jnp.concatenate([lo, hi], axis=0)

    scores = []
    for kv in range(SWA_KV_HEADS):
        k_bd = block_diag(k_all[:, kv * hd:(kv + 1) * hd], [], [])
        q3 = jnp.stack([q[:, (kv * npair + j) * 2 * hd:(kv * npair + j + 1) * 2 * hd]
                        for j in range(npair)], axis=0)
        scores.append(jnp.einsum("pqd,pkd->pqk", q3, jnp.stack([k_bd] * npair, axis=0),
                                 preferred_element_type=F32))
    for kv in range(SWA_KV_HEADS):
        v_bd = block_diag(v_all[:, kv * hd:(kv + 1) * hd], [ones, zeros], [zeros, ones])
        probs = []
        for half in range(2):
            s = scores[kv][:, :, half * nk:(half + 1) * nk]
            sink = jnp.stack([sink_ref[kv * SWA_GROUP + 2 * j + half] for j in range(npair)], axis=0)
            s = jnp.where(mask, s * (hd ** -0.5 * LOG2_E), jnp.where(is_sink, sink * LOG2_E, -jnp.inf))
            probs.append(jnp.exp2(s - jnp.max(s, axis=-1, keepdims=True)).astype(BF16))
        p = jnp.concatenate(probs, axis=-1)
        o = jnp.einsum("pqk,pkd->pqd", p, jnp.stack([v_bd] * npair, axis=0),
                       preferred_element_type=F32)
        o = o[:, :, :2 * hd] / o[:, :, 2 * hd:]
        for j in range(npair):
            c0 = (kv * npair + j) * 2 * hd
            o_ref[0, :, c0:c0 + 2 * hd] = o[j].astype(o_ref.dtype)


def _swa_attention(qkv, sinks, batch, seq):
    n = qkv.shape[1]
    qkv = qkv.reshape(batch, seq, n)
    w = SWA_WINDOW
    nq = SWA_Q_HEADS * SWA_HEAD_DIM
    nkv = SWA_KV_HEADS * SWA_HEAD_DIM
    kcol = nq // nkv
    vcol = kcol + 1
    nblk = pl.cdiv(seq, w)

    def kv_spec(col, which):
        if which == "prev":
            return pl.BlockSpec((1, w, nkv), lambda b, m: (b, jnp.maximum(m - 1, 0), col))
        if which == "cur":
            return pl.BlockSpec((1, w, nkv), lambda b, m: (b, m, col))
        return pl.BlockSpec((1, w, nkv), lambda b, m: (b, 0, col))

    out = pl.pallas_call(
        functools.partial(_swa_kernel, seq=seq),
        grid=(batch, nblk),
        in_specs=[
            pl.BlockSpec((SWA_Q_HEADS, 1, 1), lambda b, m: (0, 0, 0)),
            pl.BlockSpec((1, w, nq), lambda b, m: (b, m, 0)),
            kv_spec(kcol, "prev"), kv_spec(kcol, "cur"), kv_spec(kcol, "meta"),
            kv_spec(vcol, "prev"), kv_spec(vcol, "cur"), kv_spec(vcol, "meta"),
        ],
        out_specs=pl.BlockSpec((1, w, nq), lambda b, m: (b, m, 0)),
        out_shape=jax.ShapeDtypeStruct((batch, seq, nq), BF16),
        compiler_params=_cparams("parallel", "parallel"),
        name="swa_attention",
    )(sinks.astype(F32).reshape(SWA_Q_HEADS, 1, 1), qkv, qkv, qkv, qkv, qkv, qkv, qkv)
    return out.reshape(batch * seq, nq)


def _bmm(a, b):
    return jnp.einsum("hmk,hkn->hmn", a.astype(BF16), b.astype(BF16), preferred_element_type=F32)


def _bmm_nt(a, b):
    return jnp.einsum("hmk,hnk->hmn", a.astype(BF16), b.astype(BF16), preferred_element_type=F32)


def _bmm_tn(a, b):
    return jnp.einsum("hkm,hkn->hmn", a.astype(BF16), b.astype(BF16), preferred_element_type=F32)


def _tri_inverse(a, ii, jj):
    c = a.shape[-1]
    eye = jnp.where(ii == jj, 1.0, 0.0)
    inv = eye - jnp.where((ii >> 1) == (jj >> 1), a, 0.0)
    lg = 1
    while (1 << lg) < c:
        same_pair = (ii >> (lg + 1)) == (jj >> (lg + 1))
        off = jnp.where(same_pair & ((ii >> lg) != (jj >> lg)), a, 0.0)
        inv = inv - _bmm(inv, _bmm(off, inv))
        lg += 1
    return inv


def _conv_silu(x_ref, tail_ref, w_ref):
    x = x_ref[0]
    c = x.shape[0]
    nt = tail_ref.shape[0]
    xe = jnp.concatenate([tail_ref[...], x], axis=0)
    w = w_ref[...]
    y = x * w[DN_CONV - 1:DN_CONV, :]
    for s in range(1, DN_CONV):
        y = y + pltpu.roll(xe, s, axis=0)[nt:, :] * w[DN_CONV - 1 - s:DN_CONV - s, :]
    tail_ref[...] = x[c - nt:, :]
    return _silu(y)


def _gdn_kernel(q_ref, k_ref, v_ref, z_ref, ba_ref, cwq_ref, cwk_ref, cwv_ref, alog_ref, dtb_ref,
                ogain_ref, o_ref, s_ref, tq_ref, tk_ref, tv_ref, *, nq, seq):
    c_idx = pl.program_id(2)
    nv = 2 * nq
    c, hd = DN_CHUNK, DN_HEAD

    @pl.when(c_idx == 0)
    def _():
        s_ref[...] = jnp.zeros_like(s_ref)
        tq_ref[...] = jnp.zeros_like(tq_ref)
        tk_ref[...] = jnp.zeros_like(tk_ref)
        tv_ref[...] = jnp.zeros_like(tv_ref)

    row = c_idx * c + lax.broadcasted_iota(jnp.int32, (c, 1), 0)
    valid = row < seq
    ii = lax.broadcasted_iota(jnp.int32, (c, c), 0)
    jj = lax.broadcasted_iota(jnp.int32, (c, c), 1)
    lower = ii >= jj

    qc = _conv_silu(q_ref, tq_ref, cwq_ref)
    kc = jnp.where(valid, _conv_silu(k_ref, tk_ref, cwk_ref), 0.0)
    vc = jnp.where(valid, _conv_silu(v_ref, tv_ref, cwv_ref), 0.0)

    ba = ba_ref[0]
    beta = jnp.where(valid, jax.nn.sigmoid(ba), 0.0)
    x = ba + dtb_ref[0]
    softplus = jnp.maximum(x, 0.0) + jnp.log1p(jnp.exp(-jnp.abs(x)))
    g = jnp.where(valid, -jnp.exp(alog_ref[0]) * softplus, 0.0)
    rr = lax.broadcasted_iota(jnp.int32, g.shape, 0)
    gc = g
    s = 1
    while s < c:
        gc = gc + jnp.where(rr >= s, pltpu.roll(gc, s, axis=0), 0.0)
        s *= 2
    gct = gc.T
    glast = gc[c - 1:c, :]

    def heads(x, n):
        return jnp.stack([x[:, h * hd:(h + 1) * hd] for h in range(n)], axis=0)

    def per_v_head(x):
        return jnp.stack([x[l // 2] for l in range(nv)], axis=0)

    qh = heads(qc, nq)
    kh = heads(kc, nq)
    qh = qh * lax.rsqrt(jnp.sum(qh * qh, axis=-1, keepdims=True) + L2_EPS) * (hd ** -0.5)
    kh = kh * lax.rsqrt(jnp.sum(kh * kh, axis=-1, keepdims=True) + L2_EPS)
    kk = per_v_head(_bmm_nt(kh, kh))
    qk = per_v_head(_bmm_nt(qh, kh))
    qv = per_v_head(qh)
    kv = per_v_head(kh)
    vh = heads(vc, nv)
    zh = heads(z_ref[0], nv)
    bi = jnp.stack([beta[:, l:l + 1] for l in range(nv)], axis=0)
    gi = jnp.stack([gc[:, nv + l:nv + l + 1] for l in range(nv)], axis=0)
    gj = jnp.stack([gct[nv + l:nv + l + 1, :] for l in range(nv)], axis=0)
    gl = jnp.stack([glast[:, nv + l:nv + l + 1] for l in range(nv)], axis=0)

    decay = jnp.where(lower, jnp.exp(jnp.where(lower, gi - gj, 0.0)), 0.0)
    a = jnp.where(ii > jj, bi * kk * decay, 0.0)
    t = _tri_inverse(a, ii, jj)
    egi = jnp.exp(gi)
    kb = kv * bi
    sol = _bmm(t, jnp.concatenate([vh * bi, kb * egi], axis=-1))
    u, wmat = sol[:, :, :hd], sol[:, :, hd:]
    state = s_ref[...]
    v_new = u - _bmm(wmat, state)
    out = _bmm(qv * egi, state) + _bmm(qk * decay, v_new)
    s_ref[...] = state * jnp.exp(gl) + _bmm_tn(kv * jnp.exp(gl - gi), v_new)
    o = _rms(out, ogain_ref[...]) * _silu(zh)
    for l in range(nv):
        o_ref[0, :, l * hd:(l + 1) * hd] = o[l].astype(o_ref.dtype)


def _gdn_mix(proj, ba, conv_w, alog_g, dtb_g, o_gain, batch, seq):
    nq = DN_QK_PER_STEP
    nv = 2 * nq
    groups = DN_QK_HEADS // nq
    proj = proj.reshape(batch, seq, proj.shape[1])
    ba = ba.reshape(batch, seq, ba.shape[1])
    c = DN_CHUNK
    wq, wv = nq * DN_HEAD, nv * DN_HEAD
    k_off = DN_KEY_DIM // wq
    v_off = 2 * DN_KEY_DIM // wv
    z_off = (2 * DN_KEY_DIM + DN_VAL_DIM) // wv

    def col(width, off):
        return pl.BlockSpec((1, c, width), lambda b, g, t: (b, t, off + g))

    def cw(width, off):
        return pl.BlockSpec((DN_CONV, width), lambda b, g, t: (0, off + g))

    gate = pl.BlockSpec((1, 1, LANES), lambda b, g, t: (g, 0, 0))
    out = pl.pallas_call(
        functools.partial(_gdn_kernel, nq=nq, seq=seq),
        grid=(batch, groups, pl.cdiv(seq, c)),
        in_specs=[
            col(wq, 0), col(wq, k_off), col(wv, v_off), col(wv, z_off), col(LANES, 0),
            cw(wq, 0), cw(wq, k_off), cw(wv, v_off),
            gate, gate,
            pl.BlockSpec((1, DN_HEAD), lambda b, g, t: (0, 0)),
        ],
        out_specs=pl.BlockSpec((1, c, wv), lambda b, g, t: (b, t, g)),
        out_shape=jax.ShapeDtypeStruct((batch, seq, DN_VAL_DIM), BF16),
        scratch_shapes=[
            pltpu.VMEM((nv, DN_HEAD, DN_HEAD), F32),
            pltpu.VMEM((CONV_HALO, wq), F32),
            pltpu.VMEM((CONV_HALO, wq), F32),
            pltpu.VMEM((CONV_HALO, wv), F32),
        ],
        compiler_params=_cparams("parallel", "parallel", "arbitrary"),
        name="gdn_mix",
    )(proj, proj, proj, proj, ba, conv_w, conv_w, conv_w, alog_g, dtb_g,
      o_gain.reshape(1, DN_HEAD))
    return out.reshape(batch * seq, DN_VAL_DIM)


def _prep_gdn_weights(w_in_all, layer, a_log, dt_bias):
    nq = DN_QK_PER_STEP
    nv = 2 * nq
    groups = DN_QK_HEADS // nq
    main = 2 * DN_KEY_DIM + 2 * DN_VAL_DIM
    assert groups == 1 and main % LANES == 0 and 2 * nv <= LANES
    wt = jnp.swapaxes(w_in_all, 1, 2)
    w_ba = _cast_t_bf16(wt, layer, main, 2 * nv, LANES)
    w_main = _cast_t_bf16(wt, layer, 0, main, CAST_T_ROWS)

    def gate_rows(p):
        p = p.astype(F32).reshape(groups, nv)
        z = jnp.zeros((groups, nv), F32)
        zp = jnp.zeros((groups, LANES - 2 * nv), F32)
        return jnp.concatenate([z, p, zp], axis=1).reshape(groups, 1, LANES)

    return w_main, w_ba, gate_rows(a_log), gate_rows(dt_bias)


def _final_norm_kernel(h_ref, gain_ref, o_ref):
    o_ref[...] = _rms(h_ref[0], gain_ref[...])


def _final_norm(h, gain, batch, seq):
    d = h.shape[1]
    h = h.reshape(batch, seq, d)
    n_out = seq - N_META
    tr = 256 if n_out % 256 == 0 else n_out
    return pl.pallas_call(
        _final_norm_kernel,
        grid=(batch, n_out // tr),
        in_specs=[
            pl.BlockSpec((pl.Element(1), pl.Element(tr), pl.Element(d)),
                         lambda b, r: (b, pl.multiple_of(N_META + r * tr, 8), 0)),
            pl.BlockSpec((1, d), lambda b, r: (0, 0)),
        ],
        out_specs=pl.BlockSpec((None, tr, d), lambda b, r: (b, r, 0)),
        out_shape=jax.ShapeDtypeStruct((batch, n_out, d), F32),
        compiler_params=_cparams("parallel", "parallel"),
        name="final_norm",
    )(h, gain.reshape(1, d))


def kernel(x, meta_tokens, ffn_pre_norm, ffn_pre_w_gu, ffn_pre_w_down, mix_norm, ffn_post_norm,
           ffn_post_w_gu, ffn_post_w_down, dn_w_in, dn_conv_w, dn_a_log, dn_dt_bias, dn_out_norm,
           dn_w_out, swa_w_qkv, swa_sinks, swa_w_out, final_norm):
    batch, seq_in, d = x.shape
    seq = seq_in + N_META
    depth = ffn_pre_norm.shape[0]
    meta = jnp.broadcast_to(meta_tokens[None].astype(x.dtype), (batch, N_META, d))
    h = jnp.concatenate([meta, x], axis=1).reshape(batch * seq, d)
    tables = None
    for i in range(depth):
        j = i // 2
        h = _ffn(h, ffn_pre_norm[i], *_prep_ffn_weights(ffn_pre_w_gu, ffn_pre_w_down, i))
        if i % 2 == 0:
            w_main, w_ba, alog_g, dtb_g = _prep_gdn_weights(dn_w_in, j, dn_a_log[j], dn_dt_bias[j])
            proj, ba = _norm_proj(h, mix_norm[i], w_main, w_ba)
            o = _gdn_mix(proj, ba, dn_conv_w[j], alog_g, dtb_g, dn_out_norm[j], batch, seq)
            h = _out_proj(o, _cast_bf16(dn_w_out, j, 0, d), h)
        else:
            if tables is None:
                tables = _rope_tables(seq)
            w_qkv = _cast_bf16(swa_w_qkv, j, 0, swa_w_qkv.shape[2])
            qkv = _swa_proj(h, mix_norm[i], w_qkv, tables, seq)
            o = _swa_attention(qkv, swa_sinks[j], batch, seq)
            h = _out_proj(o, _cast_bf16(swa_w_out, j, 0, d), h)
        h = _ffn(h, ffn_post_norm[i], *_prep_ffn_weights(ffn_post_w_gu, ffn_post_w_down, i))
    return _final_norm(h, final_norm, batch, seq)
```

```python
import functools
import math

import jax
import jax.numpy as jnp
from jax import lax
from jax.experimental import pallas as pl
from jax.experimental.pallas import tpu as pltpu

F32 = jnp.float32
BF16 = jnp.bfloat16

N_META = 16
NORM_EPS = 1e-6
L2_EPS = 1e-6
FFN_RES = 0.5
DN_QK_HEADS = 16
DN_V_HEADS = 32
DN_HEAD = 128
DN_CONV = 4
DN_CHUNK = 128
DN_KEY_DIM = DN_QK_HEADS * DN_HEAD
DN_VAL_DIM = DN_V_HEADS * DN_HEAD
SWA_Q_HEADS = 32
SWA_KV_HEADS = 4
SWA_HEAD_DIM = 64
SWA_GROUP = SWA_Q_HEADS // SWA_KV_HEADS
SWA_WINDOW = 128
ROPE_THETA = 10000.0
LOG2_E = math.log2(math.e)

LANES = 128
SWA_KEYS = 3 * LANES
assert SWA_KEYS > 2 * SWA_WINDOW + N_META
VMEM_LIMIT = 56 * 1024 * 1024
ROW_TILE_MAX = 688
FF_TILE = 512
PROJ_TILE = 2048
CAST_TILE_BYTES = 6 * 1024 * 1024
CAST_T_ROWS = 512
CONV_HALO = 8
DN_QK_PER_STEP = 16


def _cparams(*sem):
    return pltpu.CompilerParams(dimension_semantics=sem, vmem_limit_bytes=VMEM_LIMIT)


def _row_tile(rows):
    best = None
    for t in range(16, min(rows, ROW_TILE_MAX) + 1, 16):
        if rows % t == 0:
            best = t
    assert best is not None, rows
    return best


def _rms(x, gain):
    return x * lax.rsqrt(jnp.mean(x * x, axis=-1, keepdims=True) + NORM_EPS) * gain


def _silu(x):
    half = 0.5 * x
    return half + half * jnp.tanh(half)


def _mm(a, b):
    return jnp.dot(a.astype(BF16), b.astype(BF16), preferred_element_type=F32)


def _ffn_kernel(h_ref, gain_ref, wg_ref, wu_ref, wd_ref, o_ref, xn_ref, *, d_ff):
    j = pl.program_id(1)

    @pl.when(j == 0)
    def _():
        h = h_ref[...]
        xn_ref[...] = _rms(h, gain_ref[...]).astype(BF16)
        o_ref[...] = h

    xn = xn_ref[...]
    g = jnp.dot(xn, wg_ref[...], preferred_element_type=F32)
    u = jnp.dot(xn, wu_ref[...], preferred_element_type=F32)
    a = _silu(g) * (FFN_RES * u)
    wd = wd_ref[...]
    if d_ff % FF_TILE:
        col = j * FF_TILE + lax.broadcasted_iota(jnp.int32, (1, FF_TILE), 1)
        a = jnp.where(col < d_ff, a, 0.0)
        row = j * FF_TILE + lax.broadcasted_iota(jnp.int32, (FF_TILE, 1), 0)
        wd = jnp.where(row < d_ff, wd, 0.0)
    o_ref[...] += jnp.dot(a.astype(BF16), wd.astype(BF16), preferred_element_type=F32)


def _ffn(h, gain, wg, wu, wd_all, layer):
    rows, d = h.shape
    d_ff = wd_all.shape[1]
    tm = _row_tile(rows)
    return pl.pallas_call(
        functools.partial(_ffn_kernel, d_ff=d_ff),
        grid=(rows // tm, pl.cdiv(d_ff, FF_TILE)),
        in_specs=[
            pl.BlockSpec((tm, d), lambda i, j: (i, 0)),
            pl.BlockSpec((1, d), lambda i, j: (0, 0)),
            pl.BlockSpec((d, FF_TILE), lambda i, j: (0, j)),
            pl.BlockSpec((d, FF_TILE), lambda i, j: (0, j)),
            pl.BlockSpec((None, FF_TILE, d), lambda i, j: (layer, j, 0)),
        ],
        out_specs=pl.BlockSpec((tm, d), lambda i, j: (i, 0)),
        out_shape=jax.ShapeDtypeStruct((rows, d), F32),
        scratch_shapes=[pltpu.VMEM((tm, d), BF16)],
        compiler_params=_cparams("parallel", "arbitrary"),
        name="ffn",
    )(h, gain.reshape(1, d), wg, wu, wd_all)


def _cast_kernel(w_ref, o_ref, *, valid_cols):
    w = w_ref[...]
    if valid_cols < w.shape[1]:
        col = lax.broadcasted_iota(jnp.int32, (1, w.shape[1]), 1)
        w = jnp.where(col < valid_cols, w, 0.0)
    o_ref[...] = w.astype(o_ref.dtype)


def _cast_bf16(w, layer, col_block, n_cols, valid_cols=None):
    rows = w.shape[1]
    valid_cols = n_cols if valid_cols is None else valid_cols
    tr = None
    for t in range(16, rows + 1, 16):
        if rows % t == 0 and t * n_cols * 4 <= CAST_TILE_BYTES:
            tr = t
    assert tr is not None, (rows, n_cols)
    return pl.pallas_call(
        functools.partial(_cast_kernel, valid_cols=valid_cols),
        grid=(rows // tr,),
        in_specs=[pl.BlockSpec((None, tr, n_cols), lambda r: (layer, r, col_block))],
        out_specs=pl.BlockSpec((tr, n_cols), lambda r: (r, 0)),
        out_shape=jax.ShapeDtypeStruct((rows, n_cols), BF16),
        compiler_params=_cparams("parallel"),
        name="cast_bf16",
    )(w)


def _cast_t_kernel(w_ref, o_ref, *, valid):
    w = w_ref[...].T
    if valid < w.shape[1]:
        col = lax.broadcasted_iota(jnp.int32, (1, w.shape[1]), 1)
        w = jnp.where(col < valid, w, 0.0)
    o_ref[...] = w.astype(o_ref.dtype)


def _cast_t_bf16(wt, layer, row_start, n_rows, tr):
    k = wt.shape[2]
    n_out = -(-n_rows // tr) * tr
    assert row_start % tr == 0
    first = row_start // tr
    valid = tr if n_rows % tr == 0 else n_rows % tr
    assert valid == tr or n_out == tr
    return pl.pallas_call(
        functools.partial(_cast_t_kernel, valid=valid),
        grid=(n_out // tr,),
        in_specs=[pl.BlockSpec((None, tr, k), lambda r: (layer, first + r, 0))],
        out_specs=pl.BlockSpec((k, tr), lambda r: (0, r)),
        out_shape=jax.ShapeDtypeStruct((k, n_out), BF16),
        compiler_params=_cparams("parallel"),
        name="cast_t_bf16",
    )(wt)


def _prep_ffn_weights(w_gu, w_down, layer):
    d_ff = w_down.shape[1]
    return _cast_bf16(w_gu, layer, 0, d_ff), _cast_bf16(w_gu, layer, 1, d_ff), w_down, layer


def _proj_kernel(h_ref, gain_ref, w_ref, ws_ref, o_ref, os_ref, xn_ref):
    @pl.when(pl.program_id(1) == 0)
    def _():
        xn = _rms(h_ref[...], gain_ref[...]).astype(BF16)
        xn_ref[...] = xn
        os_ref[...] = jnp.dot(xn, ws_ref[...], preferred_element_type=F32)

    o_ref[...] = jnp.dot(xn_ref[...], w_ref[...], preferred_element_type=F32)


def _norm_proj(h, gain, w, w_small):
    rows, d = h.shape
    n, ns = w.shape[1], w_small.shape[1]
    tm = _row_tile(rows)
    tn = PROJ_TILE
    assert n % tn == 0
    return pl.pallas_call(
        _proj_kernel,
        grid=(rows // tm, n // tn),
        in_specs=[
            pl.BlockSpec((tm, d), lambda i, j: (i, 0)),
            pl.BlockSpec((1, d), lambda i, j: (0, 0)),
            pl.BlockSpec((d, tn), lambda i, j: (0, j)),
            pl.BlockSpec((d, ns), lambda i, j: (0, 0)),
        ],
        out_specs=[
            pl.BlockSpec((tm, tn), lambda i, j: (i, j)),
            pl.BlockSpec((tm, ns), lambda i, j: (i, 0)),
        ],
        out_shape=[
            jax.ShapeDtypeStruct((rows, n), F32),
            jax.ShapeDtypeStruct((rows, ns), F32),
        ],
        scratch_shapes=[pltpu.VMEM((tm, d), BF16)],
        compiler_params=_cparams("parallel", "arbitrary"),
        name="norm_proj",
    )(h, gain.reshape(1, d), w, w_small)


def _rope_table_kernel(c_ref, s1_ref, s2_ref):
    shape = c_ref.shape
    pos = lax.broadcasted_iota(jnp.int32, shape, 0).astype(F32)
    dim = lax.broadcasted_iota(jnp.int32, shape, 1) % SWA_HEAD_DIM
    half = SWA_HEAD_DIM // 2
    k = (dim % half).astype(F32)
    inv = jnp.exp(k * (-2.0 / SWA_HEAD_DIM * math.log(ROPE_THETA)))
    ang = pos * inv
    s = jnp.sin(ang)
    c_ref[...] = jnp.cos(ang)
    s1_ref[...] = jnp.where(dim < half, -s, 0.0)
    s2_ref[...] = jnp.where(dim >= half, s, 0.0)


def _rope_tables(seq):
    shp = jax.ShapeDtypeStruct((seq, LANES), F32)
    return pl.pallas_call(_rope_table_kernel, out_shape=(shp, shp, shp), name="rope_tables")()


def _swa_proj_kernel(h_ref, gain_ref, w_ref, c_ref, s1_ref, s2_ref, o_ref, *, n_rope):
    xn = _rms(h_ref[...], gain_ref[...]).astype(BF16)
    y = jnp.dot(xn, w_ref[...], preferred_element_type=F32)
    c, s1, s2 = c_ref[...], s1_ref[...], s2_ref[...]
    half = SWA_HEAD_DIM // 2
    for ch in range(y.shape[1] // LANES):
        yc = y[:, ch * LANES:(ch + 1) * LANES]
        if ch < n_rope:
            yc = (yc * c + pltpu.roll(yc, LANES - half, axis=1) * s1
                  + pltpu.roll(yc, half, axis=1) * s2)
        o_ref[:, ch * LANES:(ch + 1) * LANES] = yc.astype(o_ref.dtype)


def _swa_proj(h, gain, w, tables, seq):
    rows, d = h.shape
    n = w.shape[1]
    tm = _row_tile(seq)
    per_seq = seq // tm
    n_rope = (SWA_Q_HEADS + SWA_KV_HEADS) * SWA_HEAD_DIM // LANES
    tab = pl.BlockSpec((tm, LANES), lambda i: (i % per_seq, 0))
    return pl.pallas_call(
        functools.partial(_swa_proj_kernel, n_rope=n_rope),
        grid=(rows // tm,),
        in_specs=[
            pl.BlockSpec((tm, d), lambda i: (i, 0)),
            pl.BlockSpec((1, d), lambda i: (0, 0)),
            pl.BlockSpec((d, n), lambda i: (0, 0)),
            tab, tab, tab,
        ],
        out_specs=pl.BlockSpec((tm, n), lambda i: (i, 0)),
        out_shape=jax.ShapeDtypeStruct((rows, n), BF16),
        compiler_params=_cparams("parallel"),
        name="swa_proj",
    )(h, gain.reshape(1, d), w, *tables)


def _out_proj_kernel(a_ref, w_ref, h_ref, o_ref):
    o_ref[...] = h_ref[...] + jnp.dot(a_ref[...], w_ref[...], preferred_element_type=F32)


def _out_proj(a, w, h, tn=1024):
    rows, k = a.shape
    d = w.shape[1]
    tm = _row_tile(rows)
    return pl.pallas_call(
        _out_proj_kernel,
        grid=(rows // tm, d // tn),
        in_specs=[
            pl.BlockSpec((tm, k), lambda i, j: (i, 0)),
            pl.BlockSpec((k, tn), lambda i, j: (0, j)),
            pl.BlockSpec((tm, tn), lambda i, j: (i, j)),
        ],
        out_specs=pl.BlockSpec((tm, tn), lambda i, j: (i, j)),
        out_shape=jax.ShapeDtypeStruct((rows, d), F32),
        compiler_params=_cparams("parallel", "arbitrary"),
        name="out_proj",
    )(a, w, h)


def _swa_kernel(sink_ref, q_ref, kp_ref, kc_ref, km_ref, vp_ref, vc_ref, vm_ref, o_ref, *, seq):
    m = pl.program_id(1)
    w, hd, nm = SWA_WINDOW, SWA_HEAD_DIM, N_META
    nk = SWA_KEYS
    n_real = 2 * w + nm
    ii = lax.broadcasted_iota(jnp.int32, (w, nk), 0)
    jj = lax.broadcasted_iota(jnp.int32, (w, nk), 1)
    band = (jj > ii) & (jj <= ii + w) & ((m > 0) | (jj >= w))
    meta = (jj >= 2 * w) & (jj < n_real) & (m * w + ii - (jj - 2 * w) >= w)
    mask = band | meta
    is_sink = lax.broadcasted_iota(jnp.int32, (1, nk), 1) == n_real
    krow = lax.broadcasted_iota(jnp.int32, (n_real, 1), 0)
    kabs = jnp.where(krow < 2 * w, (m - 1) * w + krow, krow - 2 * w)
    kvalid = (kabs >= 0) & (kabs < seq)
    k_all = jnp.concatenate([kp_ref[0], kc_ref[0], km_ref[0, :nm, :]], axis=0)
    v_all = jnp.concatenate([vp_ref[0], vc_ref[0], vm_ref[0, :nm, :]], axis=0)
    fill = jnp.zeros((nk - n_real, k_all.shape[1]), k_all.dtype)
    k_all = jnp.concatenate([jnp.where(kvalid, k_all, jnp.zeros_like(k_all)), fill], axis=0)
    v_all = jnp.concatenate([jnp.where(kvalid, v_all, jnp.zeros_like(v_all)), fill], axis=0)
    ones = jnp.ones((nk, hd), k_all.dtype)
    zeros = jnp.zeros((nk, hd), k_all.dtype)
    q = q_ref[0]
    npair = SWA_GROUP // 2

    def block_diag(x, extra_lo, extra_hi):
        lo = jnp.concatenate([x, jnp.zeros_like(x)] + extra_lo, axis=1)
        hi = jnp.concatenate([jnp.zeros_like(x), x] + extra_hi, axis=1)
        return jnp.concatenate([lo, hi], axis=0)

    scores = []
    for kv in range(SWA_KV_HEADS):
        k_bd = block_diag(k_all[:, kv * hd:(kv + 1) * hd], [], [])
        q3 = jnp.stack([q[:, (kv * npair + j) * 2 * hd:(kv * npair + j + 1) * 2 * hd]
                        for j in range(npair)], axis=0)
        scores.append(jnp.einsum("pqd,pkd->pqk", q3, jnp.stack([k_bd] * npair, axis=0),
                                 preferred_element_type=F32))
    for kv in range(SWA_KV_HEADS):
        v_bd = block_diag(v_all[:, kv * hd:(kv + 1) * hd], [ones, zeros], [zeros, ones])
        probs = []
        for half in range(2):
            s = scores[kv][:, :, half * nk:(half + 1) * nk]
            sink = jnp.stack([sink_ref[kv * SWA_GROUP + 2 * j + half] for j in range(npair)], axis=0)
            s = jnp.where(mask, s * (hd ** -0.5 * LOG2_E), jnp.where(is_sink, sink * LOG2_E, -jnp.inf))
            probs.append(jnp.exp2(s - jnp.max(s, axis=-1, keepdims=True)).astype(BF16))
        p = jnp.concatenate(probs, axis=-1)
        o = jnp.einsum("pqk,pkd->pqd", p, jnp.stack([v_bd] * npair, axis=0),
                       preferred_element_type=F32)
        o = o[:, :, :2 * hd] / o[:, :, 2 * hd:]
        for j in range(npair):
            c0 = (kv * npair + j) * 2 * hd
            o_ref[0, :, c0:c0 + 2 * hd] = o[j].astype(o_ref.dtype)


def _swa_attention(qkv, sinks, batch, seq):
    n = qkv.shape[1]
    qkv = qkv.reshape(batch, seq, n)
    w = SWA_WINDOW
    nq = SWA_Q_HEADS * SWA_HEAD_DIM
    nkv = SWA_KV_HEADS * SWA_HEAD_DIM
    kcol = nq // nkv
    vcol = kcol + 1
    nblk = pl.cdiv(seq, w)

    def kv_spec(col, which):
        if which == "prev":
            return pl.BlockSpec((1, w, nkv), lambda b, m: (b, jnp.maximum(m - 1, 0), col))
        if which == "cur":
            return pl.BlockSpec((1, w, nkv), lambda b, m: (b, m, col))
        return pl.BlockSpec((1, w, nkv), lambda b, m: (b, 0, col))

    out = pl.pallas_call(
        functools.partial(_swa_kernel, seq=seq),
        grid=(batch, nblk),
        in_specs=[
            pl.BlockSpec((SWA_Q_HEADS, 1, 1), lambda b, m: (0, 0, 0)),
            pl.BlockSpec((1, w, nq), lambda b, m: (b, m, 0)),
            kv_spec(kcol, "prev"), kv_spec(kcol, "cur"), kv_spec(kcol, "meta"),
            kv_spec(vcol, "prev"), kv_spec(vcol, "cur"), kv_spec(vcol, "meta"),
        ],
        out_specs=pl.BlockSpec((1, w, nq), lambda b, m: (b, m, 0)),
        out_shape=jax.ShapeDtypeStruct((batch, seq, nq), BF16),
        compiler_params=_cparams("parallel", "parallel"),
        name="swa_attention",
    )(sinks.astype(F32).reshape(SWA_Q_HEADS, 1, 1), qkv, qkv, qkv, qkv, qkv, qkv, qkv)
    return out.reshape(batch * seq, nq)


def _bmm(a, b):
    return jnp.einsum("hmk,hkn->hmn", a.astype(BF16), b.astype(BF16), preferred_element_type=F32)


def _bmm_nt(a, b):
    return jnp.einsum("hmk,hnk->hmn", a.astype(BF16), b.astype(BF16), preferred_element_type=F32)


def _bmm_tn(a, b):
    return jnp.einsum("hkm,hkn->hmn", a.astype(BF16), b.astype(BF16), preferred_element_type=F32)


def _tri_inverse(a, ii, jj):
    c = a.shape[-1]
    eye = jnp.where(ii == jj, 1.0, 0.0)
    inv = eye - jnp.where((ii >> 1) == (jj >> 1), a, 0.0)
    lg = 1
    while (1 << lg) < c:
        same_pair = (ii >> (lg + 1)) == (jj >> (lg + 1))
        off = jnp.where(same_pair & ((ii >> lg) != (jj >> lg)), a, 0.0)
        inv = inv - _bmm(inv, _bmm(off, inv))
        lg += 1
    return inv


def _conv_silu(x_ref, tail_ref, w_ref):
    x = x_ref[0]
    c = x.shape[0]
    nt = tail_ref.shape[0]
    xe = jnp.concatenate([tail_ref[...], x], axis=0)
    w = w_ref[...]
    y = x * w[DN_CONV - 1:DN_CONV, :]
    for s in range(1, DN_CONV):
        y = y + pltpu.roll(xe, s, axis=0)[nt:, :] * w[DN_CONV - 1 - s:DN_CONV - s, :]
    tail_ref[...] = x[c - nt:, :]
    return _silu(y)


def _gdn_kernel(q_ref, k_ref, v_ref, z_ref, ba_ref, cwq_ref, cwk_ref, cwv_ref, alog_ref, dtb_ref,
                ogain_ref, o_ref, s_ref, tq_ref, tk_ref, tv_ref, *, nq, seq):
    c_idx = pl.program_id(2)
    nv = 2 * nq
    c, hd = DN_CHUNK, DN_HEAD

    @pl.when(c_idx == 0)
    def _():
        s_ref[...] = jnp.zeros_like(s_ref)
        tq_ref[...] = jnp.zeros_like(tq_ref)
        tk_ref[...] = jnp.zeros_like(tk_ref)
        tv_ref[...] = jnp.zeros_like(tv_ref)

    row = c_idx * c + lax.broadcasted_iota(jnp.int32, (c, 1), 0)
    valid = row < seq
    ii = lax.broadcasted_iota(jnp.int32, (c, c), 0)
    jj = lax.broadcasted_iota(jnp.int32, (c, c), 1)
    lower = ii >= jj

    qc = _conv_silu(q_ref, tq_ref, cwq_ref)
    kc = jnp.where(valid, _conv_silu(k_ref, tk_ref, cwk_ref), 0.0)
    vc = jnp.where(valid, _conv_silu(v_ref, tv_ref, cwv_ref), 0.0)

    ba = ba_ref[0]
    beta = jnp.where(valid, jax.nn.sigmoid(ba), 0.0)
    x = ba + dtb_ref[0]
    softplus = jnp.maximum(x, 0.0) + jnp.log1p(jnp.exp(-jnp.abs(x)))
    g = jnp.where(valid, -jnp.exp(alog_ref[0]) * softplus, 0.0)
    rr = lax.broadcasted_iota(jnp.int32, g.shape, 0)
    gc = g
    s = 1
    while s < c:
        gc = gc + jnp.where(rr >= s, pltpu.roll(gc, s, axis=0), 0.0)
        s *= 2
    gct = gc.T
    glast = gc[c - 1:c, :]

    def heads(x, n):
        return jnp.stack([x[:, h * hd:(h + 1) * hd] for h in range(n)], axis=0)

    def per_v_head(x):
        return jnp.stack([x[l // 2] for l in range(nv)], axis=0)

    qh = heads(qc, nq)
    kh = heads(kc, nq)
    qh = qh * lax.rsqrt(jnp.sum(qh * qh, axis=-1, keepdims=True) + L2_EPS) * (hd ** -0.5)
    kh = kh * lax.rsqrt(jnp.sum(kh * kh, axis=-1, keepdims=True) + L2_EPS)
    kk = per_v_head(_bmm_nt(kh, kh))
    qk = per_v_head(_bmm_nt(qh, kh))
    qv = per_v_head(qh)
    kv = per_v_head(kh)
    vh = heads(vc, nv)
    zh = heads(z_ref[0], nv)
    bi = jnp.stack([beta[:, l:l + 1] for l in range(nv)], axis=0)
    gi = jnp.stack([gc[:, nv + l:nv + l + 1] for l in range(nv)], axis=0)
    gj = jnp.stack([gct[nv + l:nv + l + 1, :] for l in range(nv)], axis=0)
    gl = jnp.stack([glast[:, nv + l:nv + l + 1] for l in range(nv)], axis=0)

    decay = jnp.where(lower, jnp.exp(jnp.where(lower, gi - gj, 0.0)), 0.0)
    a = jnp.where(ii > jj, bi * kk * decay, 0.0)
    t = _tri_inverse(a, ii, jj)
    egi = jnp.exp(gi)
    kb = kv * bi
    sol = _bmm(t, jnp.concatenate([vh * bi, kb * egi], axis=-1))
    u, wmat = sol[:, :, :hd], sol[:, :, hd:]
    state = s_ref[...]
    v_new = u - _bmm(wmat, state)
    out = _bmm(qv * egi, state) + _bmm(qk * decay, v_new)
    s_ref[...] = state * jnp.exp(gl) + _bmm_tn(kv * jnp.exp(gl - gi), v_new)
    o = _rms(out, ogain_ref[...]) * _silu(zh)
    for l in range(nv):
        o_ref[0, :, l * hd:(l + 1) * hd] = o[l].astype(o_ref.dtype)


def _gdn_mix(proj, ba, conv_w, alog_g, dtb_g, o_gain, batch, seq):
    nq = DN_QK_PER_STEP
    nv = 2 * nq
    groups = DN_QK_HEADS // nq
    proj = proj.reshape(batch, seq, proj.shape[1])
    ba = ba.reshape(batch, seq, ba.shape[1])
    c = DN_CHUNK
    wq, wv = nq * DN_HEAD, nv * DN_HEAD
    k_off = DN_KEY_DIM // wq
    v_off = 2 * DN_KEY_DIM // wv
    z_off = (2 * DN_KEY_DIM + DN_VAL_DIM) // wv

    def col(width, off):
        return pl.BlockSpec((1, c, width), lambda b, g, t: (b, t, off + g))

    def cw(width, off):
        return pl.BlockSpec((DN_CONV, width), lambda b, g, t: (0, off + g))

    gate = pl.BlockSpec((1, 1, LANES), lambda b, g, t: (g, 0, 0))
    out = pl.pallas_call(
        functools.partial(_gdn_kernel, nq=nq, seq=seq),
        grid=(batch, groups, pl.cdiv(seq, c)),
        in_specs=[
            col(wq, 0), col(wq, k_off), col(wv, v_off), col(wv, z_off), col(LANES, 0),
            cw(wq, 0), cw(wq, k_off), cw(wv, v_off),
            gate, gate,
            pl.BlockSpec((1, DN_HEAD), lambda b, g, t: (0, 0)),
        ],
        out_specs=pl.BlockSpec((1, c, wv), lambda b, g, t: (b, t, g)),
        out_shape=jax.ShapeDtypeStruct((batch, seq, DN_VAL_DIM), BF16),
        scratch_shapes=[
            pltpu.VMEM((nv, DN_HEAD, DN_HEAD), F32),
            pltpu.VMEM((CONV_HALO, wq), F32),
            pltpu.VMEM((CONV_HALO, wq), F32),
            pltpu.VMEM((CONV_HALO, wv), F32),
        ],
        compiler_params=_cparams("parallel", "parallel", "arbitrary"),
        name="gdn_mix",
    )(proj, proj, proj, proj, ba, conv_w, conv_w, conv_w, alog_g, dtb_g,
      o_gain.reshape(1, DN_HEAD))
    return out.reshape(batch * seq, DN_VAL_DIM)


def _prep_gdn_weights(w_in_all, layer, a_log, dt_bias):
    nq = DN_QK_PER_STEP
    nv = 2 * nq
    groups = DN_QK_HEADS // nq
    main = 2 * DN_KEY_DIM + 2 * DN_VAL_DIM
    assert groups == 1 and main % LANES == 0 and 2 * nv <= LANES
    wt = jnp.swapaxes(w_in_all, 1, 2)
    w_ba = _cast_t_bf16(wt, layer, main, 2 * nv, LANES)
    w_main = _cast_t_bf16(wt, layer, 0, main, CAST_T_ROWS)

    def gate_rows(p):
        p = p.astype(F32).reshape(groups, nv)
        z = jnp.zeros((groups, nv), F32)
        zp = jnp.zeros((groups, LANES - 2 * nv), F32)
        return jnp.concatenate([z, p, zp], axis=1).reshape(groups, 1, LANES)

    return w_main, w_ba, gate_rows(a_log), gate_rows(dt_bias)


def _final_norm_kernel(h_ref, gain_ref, o_ref):
    o_ref[...] = _rms(h_ref[0], gain_ref[...])


def _final_norm(h, gain, batch, seq):
    d = h.shape[1]
    h = h.reshape(batch, seq, d)
    n_out = seq - N_META
    tr = 256 if n_out % 256 == 0 else n_out
    return pl.pallas_call(
        _final_norm_kernel,
        grid=(batch, n_out // tr),
        in_specs=[
            pl.BlockSpec((pl.Element(1), pl.Element(tr), pl.Element(d)),
                         lambda b, r: (b, pl.multiple_of(N_META + r * tr, 8), 0)),
            pl.BlockSpec((1, d), lambda b, r: (0, 0)),
        ],
        out_specs=pl.BlockSpec((None, tr, d), lambda b, r: (b, r, 0)),
        out_shape=jax.ShapeDtypeStruct((batch, n_out, d), F32),
        compiler_params=_cparams("parallel", "parallel"),
        name="final_norm",
    )(h, gain.reshape(1, d))


def kernel(x, meta_tokens, ffn_pre_norm, ffn_pre_w_gu, ffn_pre_w_down, mix_norm, ffn_post_norm,
           ffn_post_w_gu, ffn_post_w_down, dn_w_in, dn_conv_w, dn_a_log, dn_dt_bias, dn_out_norm,
           dn_w_out, swa_w_qkv, swa_sinks, swa_w_out, final_norm):
    batch, seq_in, d = x.shape
    seq = seq_in + N_META
    depth = ffn_pre_norm.shape[0]
    meta = jnp.broadcast_to(meta_tokens[None].astype(x.dtype), (batch, N_META, d))
    h = jnp.concatenate([meta, x], axis=1).reshape(batch * seq, d)
    tables = None
    for i in range(depth):
        j = i // 2
        h = _ffn(h, ffn_pre_norm[i], *_prep_ffn_weights(ffn_pre_w_gu, ffn_pre_w_down, i))
        if i % 2 == 0:
            w_main, w_ba, alog_g, dtb_g = _prep_gdn_weights(dn_w_in, j, dn_a_log[j], dn_dt_bias[j])
            proj, ba = _norm_proj(h, mix_norm[i], w_main, w_ba)
            o = _gdn_mix(proj, ba, dn_conv_w[j], alog_g, dtb_g, dn_out_norm[j], batch, seq)
            h = _out_proj(o, _cast_bf16(dn_w_out, j, 0, d), h)
        else:
            if tables is None:
                tables = _rope_tables(seq)
            w_qkv = _cast_bf16(swa_w_qkv, j, 0, swa_w_qkv.shape[2])
            qkv = _swa_proj(h, mix_norm[i], w_qkv, tables, seq)
            o = _swa_attention(qkv, swa_sinks[j], batch, seq)
            h = _out_proj(o, _cast_bf16(swa_w_out, j, 0, d), h)
        h = _ffn(h, ffn_post_norm[i], *_prep_ffn_weights(ffn_post_w_gu, ffn_post_w_down, i))
    return _final_norm(h, final_norm, batch, seq)
```
